```python
import math
import jax, jax.numpy as jnp
from jax import lax
import numpy as np

D_MODEL = 1024
BATCH = 2
SEQ = 8192
DEPTH = 1

N_MEM = 256
EPS = 1e-6
S5_WIDTH = D_MODEL // 2
S5_GROUP = 16
S5_GROUPS = S5_WIDTH // S5_GROUP
S5_STATE = 64
DT_MIN = 1e-3
DT_MAX = 1e-1
M_WIDTH = D_MODEL // 2
M_HEADS = 4
M_HEAD_DIM = M_WIDTH // M_HEADS
M_CONV = 4
M_CHUNK = 128
X_HEADS = 4
X_HEAD_DIM = D_MODEL // X_HEADS
D_FF = ((8 * D_MODEL // 3 + 255) // 256) * 256
FFN_CONV = 3
IN_WIDTHS = (S5_WIDTH, M_WIDTH, M_WIDTH, M_WIDTH, M_WIDTH, M_HEADS, M_HEADS, 2 * D_MODEL)
IN_WIDTH = sum(IN_WIDTHS)

kernel_name = "hybrid_s5_mlstm_gated_xattn_convffn"


def _rmsnorm(x, g):
    xf = x.astype(jnp.float32)
    y = xf * lax.rsqrt(jnp.mean(xf * xf, axis=-1, keepdims=True) + EPS)
    return (y * g.astype(jnp.float32)).astype(x.dtype)


def _causal_dwconv(x, w, b):
    K, C = w.shape
    y = lax.conv_general_dilated(
        x, w[:, None, :].astype(x.dtype), window_strides=(1,), padding=[(K - 1, 0)],
        dimension_numbers=('NWC', 'WIO', 'NWC'), feature_group_count=C)
    return y + b.astype(x.dtype)


def _s5(u, lam_re, lam_im, b_re, b_im, c_re, c_im, d, log_dt, w_glu, b_glu):
    Bsz, L, _ = u.shape
    f32 = jnp.float32
    lam = lax.complex(lam_re.astype(f32), lam_im.astype(f32))
    dt = jnp.exp(log_dt.astype(f32))[:, None]
    a_bar = jnp.exp(lam * dt)
    b_mat = lax.complex(b_re.astype(f32), b_im.astype(f32))
    b_bar = ((a_bar - 1.0) / lam)[..., None] * b_mat
    ug = u.astype(f32).reshape(Bsz, L, S5_GROUPS, S5_GROUP)
    bu = jnp.einsum('blgh,gph->blgp', ug.astype(jnp.complex64), b_bar)
    a_seq = jnp.broadcast_to(a_bar, bu.shape)

    def combine(left, right):
        a_l, s_l = left
        a_r, s_r = right
        return a_r * a_l, a_r * s_l + s_r

    _, states = lax.associative_scan(combine, (a_seq, bu), axis=1)
    c_mat = lax.complex(c_re.astype(f32), c_im.astype(f32))
    y = jnp.einsum('blgp,ghp->blgh', states, c_mat).real + d.astype(f32) * ug
    y = jax.nn.gelu(y.reshape(Bsz, L, S5_WIDTH))
    y = y * jax.nn.sigmoid(y @ w_glu.astype(f32) + b_glu.astype(f32))
    return y.astype(u.dtype)


def _mlstm(q_in, k_in, v_in, o_in, i_pre, f_pre, conv_w, conv_b, b_i, b_f, norm_g):
    Bsz, L, _ = v_in.shape
    H, Dh, Lc = M_HEADS, M_HEAD_DIM, M_CHUNK
    NC = L // Lc
    f32 = jnp.float32
    qk = jax.nn.silu(_causal_dwconv(jnp.concatenate([q_in, k_in], axis=-1), conv_w, conv_b))
    q_raw, k_raw = jnp.split(qk, 2, axis=-1)

    def heads(t):
        return t.astype(f32).reshape(Bsz, NC, Lc, H, Dh).transpose(0, 3, 1, 2, 4)

    def gate(t, b):
        return (t.astype(f32) + b.astype(f32)).reshape(Bsz, NC, Lc, H).transpose(0, 3, 1, 2)

    q = heads(q_raw)
    k = heads(k_raw) * (Dh ** -0.5)
    v = heads(v_in)
    ig = gate(i_pre, b_i)
    logf = jax.nn.log_sigmoid(gate(f_pre, b_f))
    bcum = jnp.cumsum(logf, axis=-1)
    g_chunk = bcum[..., -1]
    a_end = g_chunk[..., None] - bcum + ig
    m_loc = jnp.max(a_end, axis=-1)

    def step(carry, xs):
        C, n, m = carry
        k_c, v_c, a_c, g_c, ml_c = xs
        m_new = jnp.maximum(g_c + m, ml_c)
        decay = jnp.exp(g_c + m - m_new)
        w = jnp.exp(a_c - m_new[..., None])
        C_new = decay[..., None, None] * C + jnp.einsum('bhs,bhsk,bhsv->bhkv', w, k_c, v_c)
        n_new = decay[..., None] * n + jnp.einsum('bhs,bhsk->bhk', w, k_c)
        return (C_new, n_new, m_new), (C, n, m)

    init = (jnp.zeros((Bsz, H, Dh, Dh), f32), jnp.zeros((Bsz, H, Dh), f32), jnp.zeros((Bsz, H), f32))
    xs = (k.transpose(2, 0, 1, 3, 4), v.transpose(2, 0, 1, 3, 4), a_end.transpose(2, 0, 1, 3),
          g_chunk.transpose(2, 0, 1), m_loc.transpose(2, 0, 1))
    _, (C_prev, n_prev, m_prev) = lax.scan(step, init, xs)
    C_prev = C_prev.transpose(1, 2, 0, 3, 4)
    n_prev = n_prev.transpose(1, 2, 0, 3)
    m_prev = m_prev.transpose(1, 2, 0)

    inter = bcum + m_prev[..., None]
    causal = jnp.tril(jnp.ones((Lc, Lc), dtype=bool))
    dlog = bcum[..., :, None] - bcum[..., None, :] + ig[..., None, :]
    dlog = jnp.where(causal, dlog, -jnp.inf)
    m_t = jnp.maximum(inter, jnp.max(dlog, axis=-1))
    s = jnp.einsum('bhctd,bhcsd->bhcts', q, k) * jnp.exp(dlog - m_t[..., None])
    inter_w = jnp.exp(inter - m_t)
    num = (inter_w[..., None] * jnp.einsum('bhctk,bhckv->bhctv', q, C_prev)
           + jnp.einsum('bhcts,bhcsv->bhctv', s, v))
    den = inter_w * jnp.einsum('bhctk,bhck->bhct', q, n_prev) + jnp.sum(s, axis=-1)
    h = num / jnp.maximum(jnp.abs(den), jnp.exp(-m_t))[..., None]
    h = h.transpose(0, 2, 3, 1, 4).reshape(Bsz, L, H, Dh)
    h = jax.nn.sigmoid(o_in.astype(f32)).reshape(Bsz, L, H, Dh) * h
    h = h * lax.rsqrt(jnp.mean(h * h, axis=-1, keepdims=True) + EPS)
    h = h * norm_g.astype(f32).reshape(H, Dh)
    return h.reshape(Bsz, L, M_WIDTH).astype(v_in.dtype)


def _cross_attn(h, mem_n, wq, wkv, wo):
    Bsz, L, _ = h.shape
    M = mem_n.shape[1]
    q = (h @ wq).reshape(Bsz, L, X_HEADS, X_HEAD_DIM)
    k, v = jnp.split(mem_n @ wkv, 2, axis=-1)
    k = k.reshape(Bsz, M, X_HEADS, X_HEAD_DIM)
    v = v.reshape(Bsz, M, X_HEADS, X_HEAD_DIM)
    s = jnp.einsum('blhd,bmhd->bhlm', q, k).astype(jnp.float32) * (X_HEAD_DIM ** -0.5)
    p = jax.nn.softmax(s, axis=-1).astype(h.dtype)
    o = jnp.einsum('bhlm,bmhd->blhd', p, v).reshape(Bsz, L, D_MODEL)
    return o @ wo


def _conv_ffn(h, w_up, conv_w, conv_b, w_down):
    u = _causal_dwconv(h @ w_up, conv_w, conv_b)
    a, b = jnp.split(u, 2, axis=-1)
    return (jax.nn.gelu(a) * b) @ w_down


def setup_inputs(seed: int = 0) -> dict:
    key = jax.random.key(seed)
    ks = iter(jax.random.split(key, 40))
    f32 = jnp.float32

    def nrm(shape, scale):
        return jax.random.normal(next(ks), shape, f32) * scale

    def gain(shape):
        return 1.0 + nrm(shape, 0.02)

    L_, G, P, Hs = DEPTH, S5_GROUPS, S5_STATE, S5_GROUP
    lam_im_base = jnp.pi * jnp.arange(P, dtype=f32)
    f_bias_base = jnp.linspace(3.0, 6.0, M_HEADS, dtype=f32)
    return {
        "x": nrm((BATCH, SEQ, D_MODEL), 1.0),
        "mem": nrm((BATCH, N_MEM, D_MODEL), 1.0),
        "mix_norm_g": gain((L_, D_MODEL)),
        "w_in": nrm((L_, D_MODEL, IN_WIDTH), D_MODEL ** -0.5),
        "s5_lam_re": -0.5 + nrm((L_, G, P), 0.01),
        "s5_lam_im": lam_im_base + nrm((L_, G, P), 0.01),
        "s5_b_re": nrm((L_, G, P, Hs), (2 * Hs) ** -0.5),
        "s5_b_im": nrm((L_, G, P, Hs), (2 * Hs) ** -0.5),
        "s5_c_re": nrm((L_, G, Hs, P), (2 * P) ** -0.5),
        "s5_c_im": nrm((L_, G, Hs, P), (2 * P) ** -0.5),
        "s5_d": nrm((L_, G, Hs), 1.0),
        "s5_log_dt": jax.random.uniform(next(ks), (L_, G), f32, math.log(DT_MIN), math.log(DT_MAX)),
        "s5_w_glu": nrm((L_, S5_WIDTH, S5_WIDTH), S5_WIDTH ** -0.5),
        "s5_b_glu": nrm((L_, S5_WIDTH), 0.01),
        "m_conv_w": nrm((L_, M_CONV, 2 * M_WIDTH), M_CONV ** -0.5),
        "m_conv_b": nrm((L_, 2 * M_WIDTH), 0.01),
        "m_b_i": nrm((L_, M_HEADS), 0.1),
        "m_b_f": f_bias_base + nrm((L_, M_HEADS), 0.1),
        "m_norm_g": gain((L_, M_WIDTH)),
        "w_br_s5": nrm((L_, S5_WIDTH, D_MODEL), S5_WIDTH ** -0.5),
        "w_br_m": nrm((L_, M_WIDTH, D_MODEL), M_WIDTH ** -0.5),
        "b_gate": nrm((L_, 2 * D_MODEL), 0.01),
        "w_out": nrm((L_, D_MODEL, D_MODEL), D_MODEL ** -0.5),
        "x_norm_g": gain((L_, D_MODEL)),
        "mem_norm_g": gain((L_, D_MODEL)),
        "x_wq": nrm((L_, D_MODEL, D_MODEL), D_MODEL ** -0.5),
        "x_wkv": nrm((L_, D_MODEL, 2 * D_MODEL), D_MODEL ** -0.5),
        "x_wo": nrm((L_, D_MODEL, D_MODEL), D_MODEL ** -0.5),
        "f_norm_g": gain((L_, D_MODEL)),
        "f_w_up": nrm((L_, D_MODEL, 2 * D_FF), D_MODEL ** -0.5),
        "f_conv_w": nrm((L_, FFN_CONV, 2 * D_FF), FFN_CONV ** -0.5),
        "f_conv_b": nrm((L_, 2 * D_FF), 0.01),
        "f_w_down": nrm((L_, D_FF, D_MODEL), D_FF ** -0.5),
        "final_norm_g": gain((D_MODEL,)),
    }


def reference(x, mem, mix_norm_g, w_in, s5_lam_re, s5_lam_im, s5_b_re, s5_b_im, s5_c_re, s5_c_im,
              s5_d, s5_log_dt, s5_w_glu, s5_b_glu, m_conv_w, m_conv_b, m_b_i, m_b_f, m_norm_g,
              w_br_s5, w_br_m, b_gate, w_out, x_norm_g, mem_norm_g, x_wq, x_wkv, x_wo,
              f_norm_g, f_w_up, f_conv_w, f_conv_b, f_w_down, final_norm_g):
    split_points = [sum(IN_WIDTHS[:i + 1]) for i in range(len(IN_WIDTHS) - 1)]
    for l in range(DEPTH):
        h = _rmsnorm(x, mix_norm_g[l])
        proj = h @ w_in[l]
        u_s5, q_in, k_in, v_in, o_in, i_pre, f_pre, gate_pre = jnp.split(proj, split_points, axis=-1)
        y_s5 = _s5(u_s5, s5_lam_re[l], s5_lam_im[l], s5_b_re[l], s5_b_im[l], s5_c_re[l], s5_c_im[l],
                   s5_d[l], s5_log_dt[l], s5_w_glu[l], s5_b_glu[l])
        y_m = _mlstm(q_in, k_in, v_in, o_in, i_pre, f_pre, m_conv_w[l], m_conv_b[l],
                     m_b_i[l], m_b_f[l], m_norm_g[l])
        gates = jax.nn.sigmoid((gate_pre + b_gate[l]).astype(jnp.float32)).astype(x.dtype)
        g_s5, g_m = jnp.split(gates, 2, axis=-1)
        merged = g_s5 * (y_s5 @ w_br_s5[l]) + g_m * (y_m @ w_br_m[l])
        x = x + merged @ w_out[l]
        h = _rmsnorm(x, x_norm_g[l])
        x = x + _cross_attn(h, _rmsnorm(mem, mem_norm_g[l]), x_wq[l], x_wkv[l], x_wo[l])
        h = _rmsnorm(x, f_norm_g[l])
        x = x + _conv_ffn(h, f_w_up[l], f_conv_w[l], f_conv_b[l], f_w_down[l])
    return _rmsnorm(x, final_norm_g)
```

```python
import functools
import math

import jax
import jax.numpy as jnp
from jax import lax
from jax.experimental import pallas as pl
from jax.experimental.pallas import tpu as pltpu

F32 = jnp.float32
BF16 = jnp.bfloat16
HIGHEST = lax.Precision.HIGHEST

EPS = 1e-6
S5_GROUP = 16
S5_STATE = 64
S5_CHUNK = 16
M_HEADS = 4
M_CONV = 4
M_CHUNK = 128
X_HEADS = 4
FFN_CONV = 3
NEG_BIG = -1e30

TOKEN_TILE = 512
FFN_CHUNK = 256
VMEM_LIMIT = 56 * 1024 * 1024


def _dot(a, b):
    return jnp.dot(a, b, preferred_element_type=F32)


def _dot_hi(a, b):
    return jnp.dot(a, b, preferred_element_type=F32, precision=HIGHEST)


def _dot_nt(a, b):
    return lax.dot_general(a, b, (((1,), (1,)), ((), ())), preferred_element_type=F32)


def _sigmoid(x):
    return 1.0 / (1.0 + jnp.exp(-x))


def _log_sigmoid(x):
    return jnp.minimum(x, 0.0) - jnp.log(1.0 + jnp.exp(-jnp.abs(x)))


def _gelu(x):
    c = math.sqrt(2.0 / math.pi)
    return x * (0.5 * (1.0 + jnp.tanh(c * (x + 0.044715 * (x * x * x)))))


def _rmsnorm(x, g):
    return x * lax.rsqrt(jnp.mean(x * x, axis=-1, keepdims=True) + EPS) * g


def _full(shape):
    n = len(shape)
    return pl.BlockSpec(shape, lambda *_: (0,) * n)


def _params(sem):
    return pltpu.CompilerParams(dimension_semantics=sem, vmem_limit_bytes=VMEM_LIMIT)


def _in_proj_kernel(x_ref, g_ref, wu_ref, wqk_ref, wv_ref, wo_ref, wif_ref, wg_ref, bg_ref,
                    u_ref, qk_ref, v_ref, o_ref, if_ref, gt_ref):
    hb = _rmsnorm(x_ref[...], g_ref[...]).astype(BF16)
    u_ref[...] = _dot(hb, wu_ref[...]).astype(BF16)
    qk_ref[...] = _dot(hb, wqk_ref[...]).astype(BF16)
    v_ref[...] = _dot(hb, wv_ref[...]).astype(BF16)
    o_ref[...] = _sigmoid(_dot(hb, wo_ref[...])).astype(BF16)
    if_ref[...] = _dot(hb, wif_ref[...])
    gt_ref[...] = _sigmoid(_dot(hb, wg_ref[...]) + bg_ref[...]).astype(BF16)


def _in_proj(x2d, g, wu, wqk, wv, wo, wif, wg, bg):
    n, d = x2d.shape
    tm = TOKEN_TILE
    row = lambda c: pl.BlockSpec((tm, c), lambda i: (i, 0))
    outs = [(wu.shape[1], BF16), (wqk.shape[1], BF16), (wv.shape[1], BF16), (wo.shape[1], BF16),
            (wif.shape[1], F32), (wg.shape[1], BF16)]
    return pl.pallas_call(
        _in_proj_kernel,
        grid=(n // tm,),
        in_specs=[row(d), _full(g.shape), _full(wu.shape), _full(wqk.shape), _full(wv.shape),
                  _full(wo.shape), _full(wif.shape), _full(wg.shape), _full(bg.shape)],
        out_specs=[row(c) for c, _ in outs],
        out_shape=[jax.ShapeDtypeStruct((n, c), dt) for c, dt in outs],
        compiler_params=_params(("parallel",)),
        name="in_proj",
    )(x2d, g, wu, wqk, wv, wo, wif, wg, bg)


def _cpow(zr, zi, t):
    e = jnp.exp(zr * t)
    return e * jnp.cos(zi * t), e * jnp.sin(zi * t)


def _s5_prep_kernel(ldt_ref, lre_r_ref, lim_r_ref, lre_c_ref, lim_c_ref, btre_ref, btim_ref,
                    ctre_ref, ctim_ref, dpad_ref,
                    m_ref, wre_ref, wim_ref, vre_ref, vimn_ref, a16_ref):
    P, H, T = S5_STATE, S5_GROUP, S5_CHUNK
    par = lax.rem(pl.program_id(0), 2)
    dt = jnp.exp(ldt_ref[0])
    lr, li = lre_r_ref[0], lim_r_ref[0]
    zr_r, zi_r = lr * dt, li * dt
    zr_c, zi_c = lre_c_ref[0] * dt, lim_c_ref[0] * dt

    ar, ai = _cpow(zr_r, zi_r, 1.0)
    inv = 1.0 / (lr * lr + li * li)
    cr = ((ar - 1.0) * lr + ai * li) * inv
    ci = (ai * lr - (ar - 1.0) * li) * inv
    btre, btim = btre_ref[0], btim_ref[0]
    bbre = cr * btre - ci * btim
    bbim = cr * btim + ci * btre

    rr = lax.broadcasted_iota(jnp.int32, (H, T * H), 0)
    rl = lax.broadcasted_iota(jnp.int32, (H, T * H), 1)
    rep = (jnp.bitwise_and(rl, H - 1) == rr).astype(F32)
    ctre = _dot_hi(ctre_ref[0], rep)
    ctim = _dot_hi(ctim_ref[0], rep)

    lane = lax.broadcasted_iota(jnp.int32, (P, T * H), 1)
    tau = jnp.right_shift(lane, 4).astype(F32)

    e0r, e0i = _cpow(zr_c, zi_c, tau)
    qre = e0r * ctre - e0i * ctim
    qim = e0r * ctim + e0i * ctre
    kt = _dot_hi(bbre, qre) - _dot_hi(bbim, qim)
    kt = kt + jnp.where(rl == rr, dpad_ref[0], 0.0)
    for s in range(T):
        blk = kt if s == 0 else jnp.where(rl >= H * s, pltpu.roll(kt, H * s, 1), 0.0)
        m_ref[0, H * s:H * (s + 1), :] = blk.astype(BF16)

    e1r, e1i = _cpow(zr_c, zi_c, tau + 1.0)
    vre = e1r * ctre - e1i * ctim
    vim = e1r * ctim + e1i * ctre
    pr = lax.broadcasted_iota(jnp.int32, (2 * P, P), 0)
    pc = lax.broadcasted_iota(jnp.int32, (2 * P, P), 1)
    place_t = (pr == pc + P * par).astype(F32)
    vre_ref[0] = _dot_hi(place_t, vre).astype(BF16)
    vimn_ref[0] = (-_dot_hi(place_t, vim)).astype(BF16)

    srow = jnp.right_shift(lax.broadcasted_iota(jnp.int32, (T * H, P), 0), 4)
    e15r, e15i = _cpow(zr_r, zi_r, (T - 1 - srow).astype(F32))
    btr = jnp.concatenate([bbre] * T, axis=0)
    bti = jnp.concatenate([bbim] * T, axis=0)
    wre = e15r * btr - e15i * bti
    wim = e15r * bti + e15i * btr
    qr = lax.broadcasted_iota(jnp.int32, (P, 2 * P), 0)
    qc = lax.broadcasted_iota(jnp.int32, (P, 2 * P), 1)
    place = (qc == qr + P * par).astype(F32)
    wre_ref[0] = _dot_hi(wre, place).astype(BF16)
    wim_ref[0] = _dot_hi(wim, place).astype(BF16)

    a16r, a16i = _cpow(zr_r, zi_r, float(T))
    row8 = lax.broadcasted_iota(jnp.int32, (8, P), 0)
    a16 = jnp.where(row8 == 0, a16r, a16i)
    a16_ref[0] = _dot_hi(a16, place)


def _s5_prep(log_dt, lam_re, lam_im, b_re, b_im, c_re, c_im, d):
    G, P = lam_re.shape
    H, T = S5_GROUP, S5_CHUNK
    args = [
        log_dt.reshape(G, 1, 1),
        lam_re.reshape(G, 1, P), lam_im.reshape(G, 1, P),
        lam_re.reshape(G, P, 1), lam_im.reshape(G, P, 1),
        b_re.transpose(0, 2, 1), b_im.transpose(0, 2, 1),
        c_re.transpose(0, 2, 1), c_im.transpose(0, 2, 1),
        jnp.pad(d, ((0, 0), (0, T * H - H))).reshape(G, 1, T * H),
    ]
    blk = lambda a: pl.BlockSpec((1,) + a.shape[1:], lambda g: (g, 0, 0))
    outs = [((G, T * H, T * H), BF16), ((G, T * H, 2 * P), BF16), ((G, T * H, 2 * P), BF16),
            ((G, 2 * P, T * H), BF16), ((G, 2 * P, T * H), BF16), ((G, 8, 2 * P), F32)]
    return pl.pallas_call(
        _s5_prep_kernel,
        grid=(G,),
        in_specs=[blk(a) for a in args],
        out_specs=[pl.BlockSpec((1,) + s[1:], lambda g: (g, 0, 0)) for s, _ in outs],
        out_shape=[jax.ShapeDtypeStruct(s, dt) for s, dt in outs],
        compiler_params=_params(("parallel",)),
        name="s5_prep",
    )(*args)


def _s5_scan_kernel(u_ref, m_ref, wre_ref, wim_ref, vre_ref, vimn_ref, a16_ref, y_ref,
                    xre_ref, xim_ref, sre_ref, sim_ref, *, n_batch):
    ua, ub = u_ref[0], u_ref[1]
    xre_ref[...] = _dot(ua, wre_ref[0]) + _dot(ub, wre_ref[1])
    xim_ref[...] = _dot(ua, wim_ref[0]) + _dot(ub, wim_ref[1])
    a16 = a16_ref[0] + a16_ref[1]
    ar, ai = a16[0:1], a16[1:2]
    rows = u_ref.shape[1] // n_batch

    def body(c, carry):
        new = []
        for b in range(n_batch):
            sr, si = carry[2 * b], carry[2 * b + 1]
            r = b * rows + c
            sre_ref[pl.ds(r, 1), :] = sr
            sim_ref[pl.ds(r, 1), :] = si
            xr = xre_ref[pl.ds(r, 1), :]
            xi = xim_ref[pl.ds(r, 1), :]
            new.append(ar * sr - ai * si + xr)
            new.append(ar * si + ai * sr + xi)
        return tuple(new)

    zero = jnp.zeros((1, a16.shape[1]), F32)
    lax.fori_loop(0, rows, body, (zero,) * (2 * n_batch), unroll=8)

    sre = sre_ref[...].astype(BF16)
    sim = sim_ref[...].astype(BF16)
    y_ref[0] = (_dot(ua, m_ref[0]) + _dot(sre, vre_ref[0]) + _dot(sim, vimn_ref[0])).astype(BF16)
    y_ref[1] = (_dot(ub, m_ref[1]) + _dot(sre, vre_ref[1]) + _dot(sim, vimn_ref[1])).astype(BF16)


def _s5_scan(u_g, m, wre, wim, vre, vimn, a16, n_batch):
    G, R, K = u_g.shape
    P2 = wre.shape[2]
    pair = lambda a: pl.BlockSpec((2,) + a.shape[1:], lambda j: (j, 0, 0))
    return pl.pallas_call(
        functools.partial(_s5_scan_kernel, n_batch=n_batch),
        grid=(G // 2,),
        in_specs=[pair(u_g), pair(m), pair(wre), pair(wim), pair(vre), pair(vimn), pair(a16)],
        out_specs=pl.BlockSpec((2, R, K), lambda j: (j, 0, 0)),
        out_shape=jax.ShapeDtypeStruct((G, R, K), BF16),
        scratch_shapes=[pltpu.VMEM((R, P2), F32)] * 4,
        compiler_params=_params(("parallel",)),
        name="s5_scan",
    )(u_g, m, wre, wim, vre, vimn, a16)


def _mlstm_kernel(qk_ref, v_ref, o_ref, if_ref, cw_ref, cb_ref, bif_ref, ng_ref, y_ref,
                  xx_ref, cn_ref, m_ref):
    Lc = M_CHUNK
    W = v_ref.shape[1]
    Dh = W // M_HEADS

    @pl.when(pl.program_id(1) == 0)
    def _():
        xx_ref[0:8, :] = jnp.zeros((8, xx_ref.shape[1]), F32)
        cn_ref[...] = jnp.zeros(cn_ref.shape, F32)
        m_ref[...] = jnp.zeros(m_ref.shape, F32)

    x = qk_ref[...].astype(F32)
    xx_ref[8:8 + Lc, :] = x
    cw = cw_ref[...]
    acc = cb_ref[...] + cw[M_CONV - 1:M_CONV] * x
    for k in range(M_CONV - 1):
        acc = acc + cw[k:k + 1] * xx_ref[pl.ds(8 - (M_CONV - 1) + k, Lc), :]
    xx_ref[0:8, :] = x[Lc - 8:Lc]
    qk = acc * _sigmoid(acc)

    lane = lax.broadcasted_iota(jnp.int32, (Lc, Lc), 1)
    row = lax.broadcasted_iota(jnp.int32, (Lc, Lc), 0)
    pre = if_ref[...] + bif_ref[...]
    gmat = jnp.where(lane < M_HEADS, pre, jnp.where(lane < 2 * M_HEADS, _log_sigmoid(pre), 0.0))
    causal = row >= lane
    tril = causal.astype(BF16)
    g_hi = gmat.astype(BF16)
    g_r1 = gmat - g_hi.astype(F32)
    g_mid = g_r1.astype(BF16)
    g_lo = (g_r1 - g_mid.astype(F32)).astype(BF16)
    bcum = _dot(tril, g_hi) + _dot(tril, g_mid) + _dot(tril, g_lo)
    gmat_t = gmat.T
    bcum_t = bcum.T

    ones_col = (lax.broadcasted_iota(jnp.int32, (Lc, Dh), 1) == 0).astype(BF16)
    for h in range(M_HEADS):
        sl = slice(h * Dh, (h + 1) * Dh)
        q = qk[:, sl]
        k = qk[:, W + h * Dh:W + (h + 1) * Dh] * (Dh ** -0.5)
        v1 = jnp.concatenate([v_ref[:, sl], ones_col], axis=1)
        ig = gmat[:, h:h + 1]
        bc = bcum[:, M_HEADS + h:M_HEADS + h + 1]
        ig_t = gmat_t[h:h + 1, :]
        bc_t = bcum_t[M_HEADS + h:M_HEADS + h + 1, :]
        g_c = bc[Lc - 1:Lc, :]
        m_prev = m_ref[h][0:1, 0:1]
        cn_prev = cn_ref[h]

        a_end = g_c - bc + ig
        m_loc = jnp.max(a_end, axis=0, keepdims=True)
        m_new = jnp.maximum(g_c + m_prev, m_loc)
        decay = jnp.exp(g_c + m_prev - m_new)
        w_col = jnp.exp(a_end - m_new)

        inter = bc + m_prev
        dlog = jnp.where(causal, bc - bc_t + ig_t, NEG_BIG)
        m_t = jnp.maximum(inter, jnp.max(dlog, axis=1, keepdims=True))
        s_mat = _dot_nt(q.astype(BF16), k.astype(BF16)) * jnp.exp(dlog - m_t)
        inter_w = jnp.exp(inter - m_t)
        lhs = jnp.concatenate([s_mat.astype(BF16), (inter_w * q).astype(BF16)], axis=1)
        rhs = jnp.concatenate([v1, cn_prev.astype(BF16)], axis=0)
        nd = _dot(lhs, rhs)
        num = nd[:, :Dh]
        den = nd[:, Dh:Dh + 1]
        hh = num / jnp.maximum(jnp.abs(den), jnp.exp(-m_t))
        hh = o_ref[:, sl].astype(F32) * hh
        hh = hh * lax.rsqrt(jnp.mean(hh * hh, axis=-1, keepdims=True) + EPS) * ng_ref[:, sl]
        y_ref[:, sl] = hh.astype(BF16)

        kw_t = (k * w_col).T.astype(BF16)
        cn_ref[h] = decay * cn_prev + _dot(kw_t, v1)
        m_ref[h] = jnp.broadcast_to(m_new, m_ref.shape[1:])


def _mlstm(qk, v, o, gif, conv_w, conv_b, bif, norm_g, n_batch):
    n, w = v.shape
    nc = n // n_batch // M_CHUNK
    dh = w // M_HEADS
    row = lambda c: pl.BlockSpec((M_CHUNK, c), lambda b, i: (b * nc + i, 0))
    return pl.pallas_call(
        _mlstm_kernel,
        grid=(n_batch, nc),
        in_specs=[row(2 * w), row(w), row(w), row(gif.shape[1]), _full(conv_w.shape),
                  _full(conv_b.shape), _full(bif.shape), _full(norm_g.shape)],
        out_specs=row(w),
        out_shape=jax.ShapeDtypeStruct((n, w), BF16),
        scratch_shapes=[pltpu.VMEM((8 + M_CHUNK, 2 * w), F32),
                        pltpu.VMEM((M_HEADS, dh, 2 * dh), F32),
                        pltpu.VMEM((M_HEADS, 8, 128), F32)],
        compiler_params=_params(("arbitrary", "arbitrary")),
        name="mlstm",
    )(qk, v, o, gif, conv_w, conv_b, bif, norm_g)


def _mem_kv_kernel(mem_ref, g_ref, wkv_ref, kv_ref):
    kv_ref[...] = _dot(_rmsnorm(mem_ref[...], g_ref[...]).astype(BF16), wkv_ref[...]).astype(BF16)


def _mem_kv(mem2d, g, wkv):
    n, d = mem2d.shape
    return pl.pallas_call(
        _mem_kv_kernel,
        grid=(1,),
        in_specs=[_full(mem2d.shape), _full(g.shape), _full(wkv.shape)],
        out_specs=_full((n, wkv.shape[1])),
        out_shape=jax.ShapeDtypeStruct((n, wkv.shape[1]), BF16),
        compiler_params=_params(("arbitrary",)),
        name="mem_kv",
    )(mem2d, g, wkv)


def _merge_attn_kernel(x_ref, ys_ref, ym_ref, gt_ref, wglu_ref, bglu_ref, wbs_ref, wbm_ref,
                       wout_ref, xg_ref, wq_ref, kv_ref, wo_ref, x2_ref):
    d = x_ref.shape[1]
    ys = _gelu(ys_ref[...].astype(F32))
    ys = ys * _sigmoid(_dot(ys.astype(BF16), wglu_ref[...]) + bglu_ref[...])
    p_s = _dot(ys.astype(BF16), wbs_ref[...])
    p_m = _dot(ym_ref[...], wbm_ref[...])
    merged = gt_ref[:, :d].astype(F32) * p_s + gt_ref[:, d:].astype(F32) * p_m
    x1 = x_ref[...] + _dot(merged.astype(BF16), wout_ref[...])

    q = _dot(_rmsnorm(x1, xg_ref[...]).astype(BF16), wq_ref[...]).astype(BF16)
    dh = d // X_HEADS
    outs = []
    for h in range(X_HEADS):
        kh = kv_ref[0, :, h * dh:(h + 1) * dh]
        vh = kv_ref[0, :, d + h * dh:d + (h + 1) * dh]
        s = _dot_nt(q[:, h * dh:(h + 1) * dh], kh) * (dh ** -0.5)
        s = s - jnp.max(s, axis=-1, keepdims=True)
        p = jnp.exp(s)
        p = p / jnp.sum(p, axis=-1, keepdims=True)
        outs.append(_dot(p.astype(BF16), vh).astype(BF16))
    o = jnp.concatenate(outs, axis=1)
    x2_ref[...] = x1 + _dot(o, wo_ref[...])


def _merge_attn(x2d, ys, ym, gt, wglu, bglu, wbs, wbm, wout, xg, wq, kv, wo, n_batch):
    n, d = x2d.shape
    tm = TOKEN_TILE
    per_b = n // n_batch // tm
    row = lambda c: pl.BlockSpec((tm, c), lambda b, i: (b * per_b + i, 0))
    n_mem = kv.shape[1]
    return pl.pallas_call(
        _merge_attn_kernel,
        grid=(n_batch, per_b),
        in_specs=[row(d), row(ys.shape[1]), row(ym.shape[1]), row(gt.shape[1]),
                  _full(wglu.shape), _full(bglu.shape), _full(wbs.shape), _full(wbm.shape),
                  _full(wout.shape), _full(xg.shape), _full(wq.shape),
                  pl.BlockSpec((1, n_mem, kv.shape[2]), lambda b, i: (b, 0, 0)),
                  _full(wo.shape)],
        out_specs=row(d),
        out_shape=jax.ShapeDtypeStruct((n, d), F32),
        compiler_params=_params(("parallel", "parallel")),
        name="merge_attn",
    )(x2d, ys, ym, gt, wglu, bglu, wbs, wbm, wout, xg, wq, kv, wo)


def _conv_ffn_kernel(x_ref, fg_ref, wup_ref, cw_ref, cb_ref, wdn_ref, ng_ref, o_ref,
                     carry_ref, wa_ref, wb_ref, *, final_norm):
    tm = x_ref.shape[0]
    dff = wdn_ref.shape[0]
    ck = FFN_CHUNK

    @pl.when(pl.program_id(1) == 0)
    def _():
        carry_ref[...] = jnp.zeros(carry_ref.shape, F32)

    x2 = x_ref[...]
    hb = _rmsnorm(x2, fg_ref[...]).astype(BF16)

    def conv(work_ref, c0):
        u = _dot(hb, wup_ref[:, c0:c0 + ck])
        work_ref[0:8, :] = carry_ref[:, c0:c0 + ck]
        work_ref[8:8 + tm, :] = u
        carry_ref[:, c0:c0 + ck] = u[tm - 8:tm]
        acc = cb_ref[:, c0:c0 + ck] + cw_ref[FFN_CONV - 1:FFN_CONV, c0:c0 + ck] * u
        for k in range(FFN_CONV - 1):
            acc = acc + cw_ref[k:k + 1, c0:c0 + ck] * work_ref[pl.ds(8 - (FFN_CONV - 1) + k, tm), :]
        return acc

    y = x2
    for c in range(dff // ck):
        a = conv(wa_ref, c * ck)
        b = conv(wb_ref, dff + c * ck)
        y = y + _dot((_gelu(a) * b).astype(BF16), wdn_ref[c * ck:(c + 1) * ck, :])
    o_ref[...] = _rmsnorm(y, ng_ref[...]) if final_norm else y


def _conv_ffn(x2, fg, wup, cw, cb, wdn, ng, n_batch, final_norm):
    n, d = x2.shape
    tm = TOKEN_TILE
    per_b = n // n_batch // tm
    row = pl.BlockSpec((tm, d), lambda b, i: (b * per_b + i, 0))
    return pl.pallas_call(
        functools.partial(_conv_ffn_kernel, final_norm=final_norm),
        grid=(n_batch, per_b),
        in_specs=[row, _full(fg.shape), _full(wup.shape), _full(cw.shape), _full(cb.shape),
                  _full(wdn.shape), _full(ng.shape)],
        out_specs=row,
        out_shape=jax.ShapeDtypeStruct((n, d), F32),
        scratch_shapes=[pltpu.VMEM((8, wup.shape[1]), F32),
                        pltpu.VMEM((8 + tm, FFN_CHUNK), F32),
                        pltpu.VMEM((8 + tm, FFN_CHUNK), F32)],
        compiler_params=_params(("arbitrary", "arbitrary")),
        name="conv_ffn",
    )(x2, fg, wup, cw, cb, wdn, ng)


def kernel(x, mem, mix_norm_g, w_in, s5_lam_re, s5_lam_im, s5_b_re, s5_b_im, s5_c_re, s5_c_im,
           s5_d, s5_log_dt, s5_w_glu, s5_b_glu, m_conv_w, m_conv_b, m_b_i, m_b_f, m_norm_g,
           w_br_s5, w_br_m, b_gate, w_out, x_norm_g, mem_norm_g, x_wq, x_wkv, x_wo,
           f_norm_g, f_w_up, f_conv_w, f_conv_b, f_w_down, final_norm_g):
    B, L, D = x.shape
    N = B * L
    depth = w_in.shape[0]
    G = s5_lam_re.shape[1]
    s5w = G * S5_GROUP
    mw = (w_in.shape[2] - s5w - 2 * M_HEADS - 2 * D) // 4
    row = lambda a: a.reshape(1, -1)
    bf = lambda a: a.astype(BF16)

    xs = x.reshape(N, D)
    mem2d = mem.reshape(-1, D)
    for l in range(depth):
        wl = w_in[l]
        c0 = s5w
        c1 = c0 + 2 * mw
        c2 = c1 + mw
        c3 = c2 + mw
        c4 = c3 + 2 * M_HEADS
        wif = jnp.pad(wl[:, c3:c4], ((0, 0), (0, 128 - 2 * M_HEADS)))
        u, qk, v, o, gif, gates = _in_proj(
            xs, row(mix_norm_g[l]), bf(wl[:, :c0]), bf(wl[:, c0:c1]), bf(wl[:, c1:c2]),
            bf(wl[:, c2:c3]), bf(wif), bf(wl[:, c4:]), row(b_gate[l]))

        m_op, wre, wim, vre, vimn, a16 = _s5_prep(
            s5_log_dt[l], s5_lam_re[l], s5_lam_im[l], s5_b_re[l], s5_b_im[l], s5_c_re[l],
            s5_c_im[l], s5_d[l])
        u_g = (u.reshape(N // S5_CHUNK, S5_CHUNK, G, S5_GROUP).transpose(2, 0, 1, 3)
               .reshape(G, N // S5_CHUNK, S5_CHUNK * S5_GROUP))
        y_g = _s5_scan(u_g, m_op, wre, wim, vre, vimn, a16, B)
        y_s5 = (y_g.reshape(G, N // S5_CHUNK, S5_CHUNK, S5_GROUP).transpose(1, 2, 0, 3)
                .reshape(N, s5w))

        bif = jnp.pad(jnp.concatenate([m_b_i[l], m_b_f[l]]), (0, 128 - 2 * M_HEADS)).reshape(1, 128)
        y_m = _mlstm(qk, v, o, gif, m_conv_w[l], row(m_conv_b[l]), bif, row(m_norm_g[l]), B)

        kv = _mem_kv(mem2d, row(mem_norm_g[l]), bf(x_wkv[l])).reshape(B, -1, 2 * D)
        x2 = _merge_attn(xs, y_s5, y_m, gates, bf(s5_w_glu[l]), row(s5_b_glu[l]), bf(w_br_s5[l]),
                         bf(w_br_m[l]), bf(w_out[l]), row(x_norm_g[l]), bf(x_wq[l]), kv,
                         bf(x_wo[l]), B)
        xs = _conv_ffn(x2, row(f_norm_g[l]), bf(f_w_up[l]), f_conv_w[l], row(f_conv_b[l]),
                       bf(f_w_down[l]), row(final_norm_g), B, l == depth - 1)
    return xs.reshape(B, L, D)
```

```python
import functools
import math

import jax
import jax.numpy as jnp
from jax import lax
from jax.experimental import pallas as pl
from jax.experimental.pallas import tpu as pltpu

F32 = jnp.float32
BF16 = jnp.bfloat16
HIGHEST = lax.Precision.HIGHEST

EPS = 1e-6
S5_GROUP = 16
S5_STATE = 64
S5_CHUNK = 16
M_HEADS = 4
M_CONV = 4
M_CHUNK = 128
X_HEADS = 4
FFN_CONV = 3
NEG_BIG = -1e30

LANES = 128
S5_BLOCK = LANES // S5_GROUP
TOKEN_TILE = 512
FFN_CHUNK = 256
FFN_PAD = 8
VMEM_LIMIT = 56 * 1024 * 1024


def _dot(a, b):
    return jnp.dot(a, b, preferred_element_type=F32)


def _dot_hi(a, b):
    return jnp.dot(a, b, preferred_element_type=F32, precision=HIGHEST)


def _dot_nt(a, b):
    return lax.dot_general(a, b, (((1,), (1,)), ((), ())), preferred_element_type=F32)


def _sigmoid(x):
    return 1.0 / (1.0 + jnp.exp(-x))


def _log_sigmoid(x):
    return jnp.minimum(x, 0.0) - jnp.log(1.0 + jnp.exp(-jnp.abs(x)))


GELU_C1 = math.sqrt(2.0 / math.pi)
GELU_C3 = 0.044715 * GELU_C1


def _gelu(x):
    return x * (0.5 * (1.0 + jnp.tanh(x * (GELU_C1 + GELU_C3 * (x * x)))))


def _rmsnorm(x, g):
    return x * lax.rsqrt(jnp.mean(x * x, axis=-1, keepdims=True) + EPS) * g


def _cmul(ar, ai, br, bi):
    return ar * br - ai * bi, ar * bi + ai * br


def _full(shape):
    n = len(shape)
    return pl.BlockSpec(shape, lambda *_: (0,) * n)


def _params(sem):
    return pltpu.CompilerParams(dimension_semantics=sem, vmem_limit_bytes=VMEM_LIMIT)


def _in_proj_kernel(x_ref, g_ref, wu_ref, wqk_ref, wv_ref, wo_ref, wif_ref, wg_ref, bg_ref,
                    ut_ref, qk_ref, v_ref, o_ref, if_ref, gt_ref, us_ref):
    hb = _rmsnorm(x_ref[...], g_ref[...]).astype(BF16)
    u = _dot(hb, wu_ref[...])
    n_slab = u.shape[1] // LANES
    for j in range(n_slab):
        us_ref[j] = u[:, j * LANES:(j + 1) * LANES]
    rows = u.shape[0] // S5_CHUNK
    for t in range(S5_CHUNK):
        for j in range(n_slab):
            ut_ref[t, :, j * LANES:(j + 1) * LANES] = (
                us_ref[j, pl.ds(t, rows, stride=S5_CHUNK), :].astype(BF16))
    qk_ref[...] = _dot(hb, wqk_ref[...]).astype(BF16)
    v_ref[...] = _dot(hb, wv_ref[...]).astype(BF16)
    o_ref[...] = _sigmoid(_dot(hb, wo_ref[...])).astype(BF16)
    if_ref[...] = _dot(hb, wif_ref[...])
    gt_ref[...] = _sigmoid(_dot(hb, wg_ref[...]) + bg_ref[...]).astype(BF16)


def _in_proj(x2d, g, wu, wqk, wv, wo, wif, wg, bg):
    n, d = x2d.shape
    tm = TOKEN_TILE
    row = lambda c: pl.BlockSpec((tm, c), lambda i: (i, 0))
    outs = [(wqk.shape[1], BF16), (wv.shape[1], BF16), (wo.shape[1], BF16),
            (wif.shape[1], F32), (wg.shape[1], BF16)]
    s5w = wu.shape[1]
    T = S5_CHUNK
    return pl.pallas_call(
        _in_proj_kernel,
        grid=(n // tm,),
        in_specs=[row(d), _full(g.shape), _full(wu.shape), _full(wqk.shape), _full(wv.shape),
                  _full(wo.shape), _full(wif.shape), _full(wg.shape), _full(bg.shape)],
        out_specs=[pl.BlockSpec((T, tm // T, s5w), lambda i: (0, i, 0))] + [row(c) for c, _ in outs],
        out_shape=([jax.ShapeDtypeStruct((T, n // T, s5w), BF16)]
                   + [jax.ShapeDtypeStruct((n, c), dt) for c, dt in outs]),
        scratch_shapes=[pltpu.VMEM((s5w // LANES, tm, LANES), F32)],
        compiler_params=_params(("parallel",)),
        name="in_proj",
    )(x2d, g, wu, wqk, wv, wo, wif, wg, bg)


def _split3(x):
    hi = x.astype(BF16)
    r1 = x - hi.astype(F32)
    mid = r1.astype(BF16)
    lo = (r1 - mid.astype(F32)).astype(BF16)
    return hi, mid, lo


def _place(x, onehot):
    hi, mid, lo = _split3(x)
    return _dot(hi, onehot) + _dot(mid, onehot) + _dot(lo, onehot)


def _s5_prep_kernel(ldt_r_ref, lre_r_ref, lim_r_ref, ldt_c_ref, lre_c_ref, lim_c_ref,
                    btre_ref, btim_ref, ctre_ref, ctim_ref, d_ref,
                    bd_ref, w_ref, v_ref, a16_ref, pre_ref, pim_ref):
    P, H, T, M = S5_STATE, S5_GROUP, S5_CHUNK, S5_BLOCK
    MP, MH = M * P, M * H

    lr, li = lre_r_ref[0], lim_r_ref[0]
    dt_r = jnp.exp(ldt_r_ref[0])
    e_r = jnp.exp(lr * dt_r)
    ar_r, ai_r = e_r * jnp.cos(li * dt_r), e_r * jnp.sin(li * dt_r)
    dt_c = jnp.exp(ldt_c_ref[0])
    e_c = jnp.exp(lre_c_ref[0] * dt_c)
    ar_c, ai_c = e_c * jnp.cos(lim_c_ref[0] * dt_c), e_c * jnp.sin(lim_c_ref[0] * dt_c)

    t_r = lax.broadcasted_iota(jnp.int32, (P, MP), 0)
    t_l = lax.broadcasted_iota(jnp.int32, (P, MP), 1)
    tile_p = (jnp.bitwise_and(t_l, P - 1) == t_r).astype(BF16)
    b_r = lax.broadcasted_iota(jnp.int32, (MH, MP), 0)
    b_l = lax.broadcasted_iota(jnp.int32, (MH, MP), 1)
    sh_h, sh_p = H.bit_length() - 1, P.bit_length() - 1
    own_b = jnp.right_shift(b_r, sh_h) == jnp.right_shift(b_l, sh_p)
    bre = jnp.where(own_b, _place(btre_ref[0], tile_p), 0.0)
    bim = jnp.where(own_b, _place(btim_ref[0], tile_p), 0.0)
    inv = 1.0 / (lr * lr + li * li)
    cr = ((ar_r - 1.0) * lr + ai_r * li) * inv
    ci = (ai_r * lr - (ar_r - 1.0) * li) * inv
    bre, bim = _cmul(cr, ci, bre, bim)

    u_r = lax.broadcasted_iota(jnp.int32, (H, MH), 0)
    u_l = lax.broadcasted_iota(jnp.int32, (H, MH), 1)
    tile_h = (jnp.bitwise_and(u_l, H - 1) == u_r).astype(BF16)
    c_r = lax.broadcasted_iota(jnp.int32, (MP, MH), 0)
    c_l = lax.broadcasted_iota(jnp.int32, (MP, MH), 1)
    own_c = jnp.right_shift(c_r, sh_p) == jnp.right_shift(c_l, sh_h)
    cre = jnp.where(own_c, _place(ctre_ref[0], tile_h), 0.0)
    cim = jnp.where(own_c, _place(ctim_ref[0], tile_h), 0.0)

    qre, qim = cre, cim
    for tau in range(T + 1):
        pre_ref[:, tau * MH:(tau + 1) * MH] = qre
        pim_ref[:, tau * MH:(tau + 1) * MH] = qim
        if tau < T:
            qre, qim = _cmul(ar_c, ai_c, qre, qim)

    bd = _dot_hi(bre, pre_ref[:, 0:T * MH]) - _dot_hi(bim, pim_ref[:, 0:T * MH])
    d_r = lax.broadcasted_iota(jnp.int32, (MH, MH), 0)
    d_l = lax.broadcasted_iota(jnp.int32, (MH, MH), 1)
    bd_ref[0, :, 0:MH] = (bd[:, 0:MH] + jnp.where(d_r == d_l, d_ref[0], 0.0)).astype(BF16)
    bd_ref[0, :, MH:T * MH] = bd[:, MH:T * MH].astype(BF16)

    v_ref[0, 0:MP, :] = pre_ref[:, MH:(T + 1) * MH].astype(BF16)
    v_ref[0, MP:2 * MP, :] = (-pim_ref[:, MH:(T + 1) * MH]).astype(BF16)

    wre, wim = bre, bim
    for t in range(T - 1, -1, -1):
        w_ref[0, t * MH:(t + 1) * MH, 0:MP] = wre.astype(BF16)
        w_ref[0, t * MH:(t + 1) * MH, MP:2 * MP] = wim.astype(BF16)
        if t > 0:
            wre, wim = _cmul(ar_r, ai_r, wre, wim)

    sr, si = ar_r, ai_r
    for _ in range(4):
        sr, si = _cmul(sr, si, sr, si)
    row8 = lax.broadcasted_iota(jnp.int32, (8, MP), 0)
    a16_ref[0] = jnp.where(row8 == 0, sr, si)


def _s5_prep(log_dt, lam_re, lam_im, b_re, b_im, c_re, c_im, d):
    G, P = lam_re.shape
    H, T, M = S5_GROUP, S5_CHUNK, S5_BLOCK
    J = G // M
    MP, MH = M * P, M * H
    ldt = jnp.repeat(log_dt, P)
    args = [
        ldt.reshape(J, 1, MP), lam_re.reshape(J, 1, MP), lam_im.reshape(J, 1, MP),
        ldt.reshape(J, MP, 1), lam_re.reshape(J, MP, 1), lam_im.reshape(J, MP, 1),
        b_re.transpose(0, 2, 1).reshape(J, MH, P), b_im.transpose(0, 2, 1).reshape(J, MH, P),
        c_re.transpose(0, 2, 1).reshape(J, MP, H), c_im.transpose(0, 2, 1).reshape(J, MP, H),
        d.reshape(J, 1, MH),
    ]
    blk = lambda s: pl.BlockSpec((1,) + tuple(s[1:]), lambda j: (j, 0, 0))
    outs = [((J, MH, T * MH), BF16), ((J, T * MH, 2 * MP), BF16), ((J, 2 * MP, T * MH), BF16),
            ((J, 8, MP), F32)]
    return pl.pallas_call(
        _s5_prep_kernel,
        grid=(J,),
        in_specs=[blk(a.shape) for a in args],
        out_specs=[blk(s) for s, _ in outs],
        out_shape=[jax.ShapeDtypeStruct(s, dt) for s, dt in outs],
        scratch_shapes=[pltpu.VMEM((MP, (T + 1) * MH), F32)] * 2,
        compiler_params=_params(("parallel",)),
        name="s5_prep",
    )(*args)


def _s5_scan_kernel(ut_ref, bd_ref, w_ref, v_ref, a16_ref, yt_ref, big_ref, xs_ref, sp_ref):
    T = S5_CHUNK
    R = ut_ref.shape[1]
    MH = ut_ref.shape[2]
    MP = a16_ref.shape[2]

    @pl.when(pl.program_id(1) == 0)
    def _():
        for s in range(T):
            for t in range(T):
                blk = (bd_ref[0, :, (t - s) * MH:(t - s + 1) * MH] if t >= s
                       else jnp.zeros((MH, MH), BF16))
                big_ref[s * MH:(s + 1) * MH, t * MH:(t + 1) * MH] = blk

    x = jnp.concatenate([ut_ref[t] for t in range(T)], axis=1)
    xs_ref[...] = _dot(x, w_ref[0])
    a16 = a16_ref[0]
    ar, ai = a16[0:1], a16[1:2]

    def body(c, carry):
        sr, si = carry
        sp_ref[pl.ds(c, 1), 0:MP] = sr
        sp_ref[pl.ds(c, 1), MP:2 * MP] = si
        xr = xs_ref[pl.ds(c, 1), 0:MP]
        xi = xs_ref[pl.ds(c, 1), MP:2 * MP]
        return ar * sr - ai * si + xr, ar * si + ai * sr + xi

    zero = jnp.zeros((1, MP), F32)
    lax.fori_loop(0, R, body, (zero, zero), unroll=8)

    sp = sp_ref[...].astype(BF16)
    for p in range(T // 2):
        k = 2 * (p + 1) * MH
        cols = slice(2 * p * MH, 2 * (p + 1) * MH)
        acc = _dot(x[:, 0:k], big_ref[0:k, cols]) + _dot(sp, v_ref[0, :, cols])
        yt_ref[2 * p] = acc[:, 0:MH].astype(BF16)
        yt_ref[2 * p + 1] = acc[:, MH:2 * MH].astype(BF16)


def _s5_scan(ut, bd, w, v, a16, n_batch):
    T, nrow, s5w = ut.shape
    R = nrow // n_batch
    MH = LANES
    MP = a16.shape[2]
    per_j = lambda s: pl.BlockSpec((1,) + tuple(s[1:]), lambda j, b: (j, 0, 0))
    io = pl.BlockSpec((T, R, MH), lambda j, b: (0, b, j))
    return pl.pallas_call(
        _s5_scan_kernel,
        grid=(s5w // MH, n_batch),
        in_specs=[io, per_j(bd.shape), per_j(w.shape), per_j(v.shape), per_j(a16.shape)],
        out_specs=io,
        out_shape=jax.ShapeDtypeStruct(ut.shape, BF16),
        scratch_shapes=[pltpu.VMEM((T * MH, T * MH), BF16),
                        pltpu.VMEM((R, 2 * MP), F32),
                        pltpu.VMEM((R, 2 * MP), F32)],
        compiler_params=_params(("arbitrary", "arbitrary")),
        name="s5_scan",
    )(ut, bd, w, v, a16)


def _mlstm_kernel(qk_ref, v_ref, o_ref, if_ref, cw_ref, cb_ref, bif_ref, ng_ref, y_ref,
                  xx_ref, cn_ref, m_ref):
    Lc = M_CHUNK
    W = v_ref.shape[1]
    Dh = W // M_HEADS

    @pl.when(pl.program_id(1) == 0)
    def _():
        xx_ref[0:8, :] = jnp.zeros((8, xx_ref.shape[1]), F32)
        cn_ref[...] = jnp.zeros(cn_ref.shape, F32)
        m_ref[...] = jnp.zeros(m_ref.shape, F32)

    x = qk_ref[...].astype(F32)
    xx_ref[8:8 + Lc, :] = x
    cw = cw_ref[...]
    acc = cb_ref[...] + cw[M_CONV - 1:M_CONV] * x
    for k in range(M_CONV - 1):
        acc = acc + cw[k:k + 1] * xx_ref[pl.ds(8 - (M_CONV - 1) + k, Lc), :]
    xx_ref[0:8, :] = x[Lc - 8:Lc]
    qk = acc * _sigmoid(acc)

    lane = lax.broadcasted_iota(jnp.int32, (Lc, Lc), 1)
    row = lax.broadcasted_iota(jnp.int32, (Lc, Lc), 0)
    pre = if_ref[...] + bif_ref[...]
    gmat = jnp.where(lane < M_HEADS, pre, jnp.where(lane < 2 * M_HEADS, _log_sigmoid(pre), 0.0))
    causal = row >= lane
    bcum = _place_left(causal.astype(BF16), gmat)
    gmat_t = gmat.T
    bcum_t = bcum.T

    ones_col = (lax.broadcasted_iota(jnp.int32, (Lc, Dh), 1) == 0).astype(BF16)
    for h in range(M_HEADS):
        sl = slice(h * Dh, (h + 1) * Dh)
        q = qk[:, sl]
        k = qk[:, W + h * Dh:W + (h + 1) * Dh] * (Dh ** -0.5)
        v1 = jnp.concatenate([v_ref[:, sl], ones_col], axis=1)
        ig = gmat[:, h:h + 1]
        bc = bcum[:, M_HEADS + h:M_HEADS + h + 1]
        ig_t = gmat_t[h:h + 1, :]
        bc_t = bcum_t[M_HEADS + h:M_HEADS + h + 1, :]
        g_c = bc[Lc - 1:Lc, :]
        m_prev = m_ref[h][0:1, 0:1]
        cn_prev = cn_ref[h]

        a_end = g_c - bc + ig
        m_loc = jnp.max(a_end, axis=0, keepdims=True)
        m_new = jnp.maximum(g_c + m_prev, m_loc)
        decay = jnp.exp(g_c + m_prev - m_new)
        w_col = jnp.exp(a_end - m_new)

        inter = bc + m_prev
        dlog = jnp.where(causal, bc - bc_t + ig_t, NEG_BIG)
        m_t = jnp.maximum(inter, jnp.max(dlog, axis=1, keepdims=True))
        s_mat = _dot_nt(q.astype(BF16), k.astype(BF16)) * jnp.exp(dlog - m_t)
        inter_w = jnp.exp(inter - m_t)
        lhs = jnp.concatenate([s_mat.astype(BF16), (inter_w * q).astype(BF16)], axis=1)
        rhs = jnp.concatenate([v1, cn_prev.astype(BF16)], axis=0)
        nd = _dot(lhs, rhs)
        num = nd[:, :Dh]
        den = nd[:, Dh:Dh + 1]
        hh = num / jnp.maximum(jnp.abs(den), jnp.exp(-m_t))
        hh = o_ref[:, sl].astype(F32) * hh
        hh = hh * lax.rsqrt(jnp.mean(hh * hh, axis=-1, keepdims=True) + EPS) * ng_ref[:, sl]
        y_ref[:, sl] = hh.astype(BF16)

        kw_t = (k * w_col).T.astype(BF16)
        cn_ref[h] = decay * cn_prev + _dot(kw_t, v1)
        m_ref[h] = jnp.broadcast_to(m_new, m_ref.shape[1:])


def _place_left(onehot, x):
    hi, mid, lo = _split3(x)
    return _dot(onehot, hi) + _dot(onehot, mid) + _dot(onehot, lo)


def _mlstm(qk, v, o, gif, conv_w, conv_b, bif, norm_g, n_batch):
    n, w = v.shape
    nc = n // n_batch // M_CHUNK
    dh = w // M_HEADS
    row = lambda c: pl.BlockSpec((M_CHUNK, c), lambda b, i: (b * nc + i, 0))
    return pl.pallas_call(
        _mlstm_kernel,
        grid=(n_batch, nc),
        in_specs=[row(2 * w), row(w), row(w), row(gif.shape[1]), _full(conv_w.shape),
                  _full(conv_b.shape), _full(bif.shape), _full(norm_g.shape)],
        out_specs=row(w),
        out_shape=jax.ShapeDtypeStruct((n, w), BF16),
        scratch_shapes=[pltpu.VMEM((8 + M_CHUNK, 2 * w), F32),
                        pltpu.VMEM((M_HEADS, dh, 2 * dh), F32),
                        pltpu.VMEM((M_HEADS, 8, LANES), F32)],
        compiler_params=_params(("arbitrary", "arbitrary")),
        name="mlstm",
    )(qk, v, o, gif, conv_w, conv_b, bif, norm_g)


def _mem_kv_kernel(mem_ref, g_ref, wkv_ref, kv_ref):
    kv_ref[...] = _dot(_rmsnorm(mem_ref[...], g_ref[...]).astype(BF16), wkv_ref[...]).astype(BF16)


def _mem_kv(mem2d, g, wkv):
    n, d = mem2d.shape
    return pl.pallas_call(
        _mem_kv_kernel,
        grid=(1,),
        in_specs=[_full(mem2d.shape), _full(g.shape), _full(wkv.shape)],
        out_specs=_full((n, wkv.shape[1])),
        out_shape=jax.ShapeDtypeStruct((n, wkv.shape[1]), BF16),
        compiler_params=_params(("arbitrary",)),
        name="mem_kv",
    )(mem2d, g, wkv)


def _merge_attn_kernel(x_ref, yt_ref, ym_ref, gt_ref, wglu_ref, bglu_ref, wbs_ref, wbm_ref,
                       wout_ref, xg_ref, wq_ref, kv_ref, wo_ref, x2_ref, ys_ref):
    d = x_ref.shape[1]
    rows = yt_ref.shape[1]
    n_slab = yt_ref.shape[2] // LANES
    for t in range(S5_CHUNK):
        for j in range(n_slab):
            ys_ref[j, pl.ds(t, rows, stride=S5_CHUNK), :] = (
                yt_ref[t, :, j * LANES:(j + 1) * LANES].astype(F32))
    ys = _gelu(jnp.concatenate([ys_ref[j] for j in range(n_slab)], axis=1))
    ys = ys * _sigmoid(_dot(ys.astype(BF16), wglu_ref[...]) + bglu_ref[...])
    p_s = _dot(ys.astype(BF16), wbs_ref[...])
    p_m = _dot(ym_ref[...], wbm_ref[...])
    merged = gt_ref[:, :d].astype(F32) * p_s + gt_ref[:, d:].astype(F32) * p_m
    x1 = x_ref[...] + _dot(merged.astype(BF16), wout_ref[...])

    q = _dot(_rmsnorm(x1, xg_ref[...]).astype(BF16), wq_ref[...]).astype(BF16)
    dh = d // X_HEADS
    outs = []
    for h in range(X_HEADS):
        kh = kv_ref[0, :, h * dh:(h + 1) * dh]
        vh = kv_ref[0, :, d + h * dh:d + (h + 1) * dh]
        s = _dot_nt(q[:, h * dh:(h + 1) * dh], kh) * (dh ** -0.5)
        s = s - jnp.max(s, axis=-1, keepdims=True)
        p = jnp.exp(s)
        p = p / jnp.sum(p, axis=-1, keepdims=True)
        outs.append(_dot(p.astype(BF16), vh).astype(BF16))
    o = jnp.concatenate(outs, axis=1)
    x2_ref[...] = x1 + _dot(o, wo_ref[...])


def _merge_attn(x2d, yt, ym, gt, wglu, bglu, wbs, wbm, wout, xg, wq, kv, wo, n_batch):
    n, d = x2d.shape
    tm = TOKEN_TILE
    per_b = n // n_batch // tm
    row = lambda c: pl.BlockSpec((tm, c), lambda b, i: (b * per_b + i, 0))
    n_mem = kv.shape[1]
    T, _, s5w = yt.shape
    return pl.pallas_call(
        _merge_attn_kernel,
        grid=(n_batch, per_b),
        in_specs=[row(d), pl.BlockSpec((T, tm // T, s5w), lambda b, i: (0, b * per_b + i, 0)),
                  row(ym.shape[1]), row(gt.shape[1]),
                  _full(wglu.shape), _full(bglu.shape), _full(wbs.shape), _full(wbm.shape),
                  _full(wout.shape), _full(xg.shape), _full(wq.shape),
                  pl.BlockSpec((1, n_mem, kv.shape[2]), lambda b, i: (b, 0, 0)),
                  _full(wo.shape)],
        out_specs=row(d),
        out_shape=jax.ShapeDtypeStruct((n, d), F32),
        scratch_shapes=[pltpu.VMEM((s5w // LANES, tm, LANES), F32)],
        compiler_params=_params(("parallel", "parallel")),
        name="merge_attn",
    )(x2d, yt, ym, gt, wglu, bglu, wbs, wbm, wout, xg, wq, kv, wo)


def _conv_ffn_kernel(x_ref, fg_ref, wup_ref, cw_ref, cb_ref, wdn_ref, ng_ref, o_ref,
                     up_ref, *, final_norm):
    tm = x_ref.shape[0]
    dff = wdn_ref.shape[0]
    ck = FFN_CHUNK
    n_chunks = dff // ck
    spc = ck // LANES
    base = 2 * FFN_PAD

    @pl.when(pl.program_id(1) == 0)
    def _():
        up_ref[:, 0:base, :] = jnp.zeros((up_ref.shape[0], base, LANES), F32)

    @pl.when(pl.program_id(1) > 0)
    def _():
        up_ref[:, 0:base, :] = up_ref[:, 2 * tm:2 * tm + base, :]

    x2 = x_ref[...]
    hb = _rmsnorm(x2, fg_ref[...]).astype(BF16)

    def up(c, half):
        c0 = half * dff + c * ck
        u = _dot(hb, wup_ref[:, c0:c0 + ck])
        for s in range(spc):
            up_ref[c * spc + s, pl.ds(base + half, tm, stride=2), :] = u[:, s * LANES:(s + 1) * LANES]

    def conv(c, half):
        c0 = half * dff + c * ck
        cols = []
        for s in range(spc):
            l0 = c0 + s * LANES
            acc = cb_ref[:, l0:l0 + LANES]
            for k in range(FFN_CONV):
                start = base + half - 2 * (FFN_CONV - 1 - k)
                acc = acc + (cw_ref[k:k + 1, l0:l0 + LANES]
                             * up_ref[c * spc + s, pl.ds(start, tm, stride=2), :])
            cols.append(acc)
        return jnp.concatenate(cols, axis=1)

    y = x2
    up(0, 0)
    up(0, 1)
    for c in range(n_chunks):
        if c + 1 < n_chunks:
            up(c + 1, 0)
            up(c + 1, 1)
        a = conv(c, 0)
        ab = a * conv(c, 1)
        th = jnp.tanh(a * (GELU_C1 + GELU_C3 * (a * a)))
        act = (ab + ab * th).astype(BF16)
        y = y + _dot(act, wdn_ref[c * ck:(c + 1) * ck, :])
    o_ref[...] = _rmsnorm(y, ng_ref[...]) if final_norm else y


def _conv_ffn(x2, fg, wup, cw, cb, wdn, ng, n_batch, final_norm):
    n, d = x2.shape
    tm = TOKEN_TILE
    per_b = n // n_batch // tm
    row = pl.BlockSpec((tm, d), lambda b, i: (b * per_b + i, 0))
    return pl.pallas_call(
        functools.partial(_conv_ffn_kernel, final_norm=final_norm),
        grid=(n_batch, per_b),
        in_specs=[row, _full(fg.shape), _full(wup.shape), _full(cw.shape), _full(cb.shape),
                  _full(wdn.shape), _full(ng.shape)],
        out_specs=row,
        out_shape=jax.ShapeDtypeStruct((n, d), F32),
        scratch_shapes=[pltpu.VMEM((wdn.shape[0] // LANES, 2 * (FFN_PAD + tm), LANES), F32)],
        compiler_params=_params(("arbitrary", "arbitrary")),
        name="conv_ffn",
    )(x2, fg, wup, cw, cb, wdn, ng)


def kernel(x, mem, mix_norm_g, w_in, s5_lam_re, s5_lam_im, s5_b_re, s5_b_im, s5_c_re, s5_c_im,
           s5_d, s5_log_dt, s5_w_glu, s5_b_glu, m_conv_w, m_conv_b, m_b_i, m_b_f, m_norm_g,
           w_br_s5, w_br_m, b_gate, w_out, x_norm_g, mem_norm_g, x_wq, x_wkv, x_wo,
           f_norm_g, f_w_up, f_conv_w, f_conv_b, f_w_down, final_norm_g):
    B, L, D = x.shape
    N = B * L
    depth = w_in.shape[0]
    G = s5_lam_re.shape[1]
    s5w = G * S5_GROUP
    mw = (w_in.shape[2] - s5w - 2 * M_HEADS - 2 * D) // 4
    row = lambda a: a.reshape(1, -1)
    bf = lambda a: a.astype(BF16)

    xs = x.reshape(N, D)
    mem2d = mem.reshape(-1, D)
    for l in range(depth):
        wl = w_in[l]
        c0 = s5w
        c1 = c0 + 2 * mw
        c2 = c1 + mw
        c3 = c2 + mw
        c4 = c3 + 2 * M_HEADS
        wif = jnp.pad(wl[:, c3:c4], ((0, 0), (0, LANES - 2 * M_HEADS)))
        ut, qk, v, o, gif, gates = _in_proj(
            xs, row(mix_norm_g[l]), bf(wl[:, :c0]), bf(wl[:, c0:c1]), bf(wl[:, c1:c2]),
            bf(wl[:, c2:c3]), bf(wif), bf(wl[:, c4:]), row(b_gate[l]))

        bd, w_st, v_st, a16 = _s5_prep(
            s5_log_dt[l], s5_lam_re[l], s5_lam_im[l], s5_b_re[l], s5_b_im[l], s5_c_re[l],
            s5_c_im[l], s5_d[l])
        yt = _s5_scan(ut, bd, w_st, v_st, a16, B)

        bif = jnp.pad(jnp.concatenate([m_b_i[l], m_b_f[l]]), (0, LANES - 2 * M_HEADS)).reshape(1, LANES)
        y_m = _mlstm(qk, v, o, gif, m_conv_w[l], row(m_conv_b[l]), bif, row(m_norm_g[l]), B)

        kv = _mem_kv(mem2d, row(mem_norm_g[l]), bf(x_wkv[l])).reshape(B, -1, 2 * D)
        x2 = _merge_attn(xs, yt, y_m, gates, bf(s5_w_glu[l]), row(s5_b_glu[l]), bf(w_br_s5[l]),
                         bf(w_br_m[l]), bf(w_out[l]), row(x_norm_g[l]), bf(x_wq[l]), kv,
                         bf(x_wo[l]), B)
        dff = f_w_down.shape[1]
        half = jnp.concatenate([jnp.ones((dff,), F32), jnp.full((dff,), 0.5, F32)])
        xs = _conv_ffn(x2, row(f_norm_g[l]), bf(f_w_up[l]), f_conv_w[l] * half,
                       row(f_conv_b[l] * half), bf(f_w_down[l]), row(final_norm_g), B,
                       l == depth - 1)
    return xs.reshape(B, L, D)
```

```python
import functools
import math

import jax
import jax.numpy as jnp
from jax import lax
from jax.experimental import pallas as pl
from jax.experimental.pallas import tpu as pltpu

F32 = jnp.float32
BF16 = jnp.bfloat16
HIGHEST = lax.Precision.HIGHEST

EPS = 1e-6
S5_GROUP = 16
S5_STATE = 64
S5_CHUNK = 16
M_HEADS = 4
M_CONV = 4
M_CHUNK = 128
M_STEP_CHUNKS = 4
M_PAD = 8
X_HEADS = 4
FFN_CONV = 3
NEG_BIG = -1e30

LANES = 128
S5_BLOCK = LANES // S5_GROUP
TOKEN_TILE = 512
FFN_CHUNK = 256
FFN_PAD = 8
VMEM_LIMIT = 56 * 1024 * 1024


def _dot(a, b):
    return jnp.dot(a, b, preferred_element_type=F32)


def _dot_hi(a, b):
    return jnp.dot(a, b, preferred_element_type=F32, precision=HIGHEST)


def _dot_nt(a, b):
    return lax.dot_general(a, b, (((1,), (1,)), ((), ())), preferred_element_type=F32)


def _sigmoid(x):
    return 1.0 / (1.0 + jnp.exp(-x))


def _log_sigmoid(x):
    return jnp.minimum(x, 0.0) - jnp.log(1.0 + jnp.exp(-jnp.abs(x)))


GELU_C1 = math.sqrt(2.0 / math.pi)
GELU_C3 = 0.044715 * GELU_C1


def _gelu(x):
    return x * (0.5 * (1.0 + jnp.tanh(x * (GELU_C1 + GELU_C3 * (x * x)))))


def _rmsnorm(x, g):
    return x * lax.rsqrt(jnp.mean(x * x, axis=-1, keepdims=True) + EPS) * g


def _cmul(ar, ai, br, bi):
    return ar * br - ai * bi, ar * bi + ai * br


def _full(shape):
    n = len(shape)
    return pl.BlockSpec(shape, lambda *_: (0,) * n)


def _params(sem):
    return pltpu.CompilerParams(dimension_semantics=sem, vmem_limit_bytes=VMEM_LIMIT)


def _in_proj_kernel(x_ref, g_ref, wu_ref, wqk_ref, wv_ref, wo_ref, wif_ref, wg_ref, bg_ref,
                    ut_ref, qk_ref, v_ref, o_ref, if_ref, gt_ref, us_ref):
    hb = _rmsnorm(x_ref[...], g_ref[...]).astype(BF16)
    u = _dot(hb, wu_ref[...])
    n_slab = u.shape[1] // LANES
    for j in range(n_slab):
        us_ref[j] = u[:, j * LANES:(j + 1) * LANES]
    rows = u.shape[0] // S5_CHUNK
    for t in range(S5_CHUNK):
        for j in range(n_slab):
            ut_ref[t, :, j * LANES:(j + 1) * LANES] = (
                us_ref[j, pl.ds(t, rows, stride=S5_CHUNK), :].astype(BF16))
    qk_ref[...] = _dot(hb, wqk_ref[...]).astype(BF16)
    v_ref[...] = _dot(hb, wv_ref[...]).astype(BF16)
    o_ref[...] = _sigmoid(_dot(hb, wo_ref[...])).astype(BF16)
    if_ref[...] = _dot(hb, wif_ref[...])
    gt_ref[...] = _sigmoid(_dot(hb, wg_ref[...]) + bg_ref[...]).astype(BF16)


def _in_proj(x2d, g, wu, wqk, wv, wo, wif, wg, bg):
    n, d = x2d.shape
    tm = TOKEN_TILE
    row = lambda c: pl.BlockSpec((tm, c), lambda i: (i, 0))
    outs = [(wqk.shape[1], BF16), (wv.shape[1], BF16), (wo.shape[1], BF16),
            (wif.shape[1], F32), (wg.shape[1], BF16)]
    s5w = wu.shape[1]
    T = S5_CHUNK
    return pl.pallas_call(
        _in_proj_kernel,
        grid=(n // tm,),
        in_specs=[row(d), _full(g.shape), _full(wu.shape), _full(wqk.shape), _full(wv.shape),
                  _full(wo.shape), _full(wif.shape), _full(wg.shape), _full(bg.shape)],
        out_specs=[pl.BlockSpec((T, tm // T, s5w), lambda i: (0, i, 0))] + [row(c) for c, _ in outs],
        out_shape=([jax.ShapeDtypeStruct((T, n // T, s5w), BF16)]
                   + [jax.ShapeDtypeStruct((n, c), dt) for c, dt in outs]),
        scratch_shapes=[pltpu.VMEM((s5w // LANES, tm, LANES), F32)],
        compiler_params=_params(("parallel",)),
        name="in_proj",
    )(x2d, g, wu, wqk, wv, wo, wif, wg, bg)


def _split3(x):
    hi = x.astype(BF16)
    r1 = x - hi.astype(F32)
    mid = r1.astype(BF16)
    lo = (r1 - mid.astype(F32)).astype(BF16)
    return hi, mid, lo


def _place(x, onehot):
    hi, mid, lo = _split3(x)
    return _dot(hi, onehot) + _dot(mid, onehot) + _dot(lo, onehot)


def _s5_prep_kernel(ldt_r_ref, lre_r_ref, lim_r_ref, ldt_c_ref, lre_c_ref, lim_c_ref,
                    btre_ref, btim_ref, ctre_ref, ctim_ref, d_ref,
                    bd_ref, w_ref, v_ref, a16_ref, pre_ref, pim_ref):
    P, H, T, M = S5_STATE, S5_GROUP, S5_CHUNK, S5_BLOCK
    MP, MH = M * P, M * H

    lr, li = lre_r_ref[0], lim_r_ref[0]
    dt_r = jnp.exp(ldt_r_ref[0])
    e_r = jnp.exp(lr * dt_r)
    ar_r, ai_r = e_r * jnp.cos(li * dt_r), e_r * jnp.sin(li * dt_r)
    dt_c = jnp.exp(ldt_c_ref[0])
    e_c = jnp.exp(lre_c_ref[0] * dt_c)
    ar_c, ai_c = e_c * jnp.cos(lim_c_ref[0] * dt_c), e_c * jnp.sin(lim_c_ref[0] * dt_c)

    t_r = lax.broadcasted_iota(jnp.int32, (P, MP), 0)
    t_l = lax.broadcasted_iota(jnp.int32, (P, MP), 1)
    tile_p = (jnp.bitwise_and(t_l, P - 1) == t_r).astype(BF16)
    b_r = lax.broadcasted_iota(jnp.int32, (MH, MP), 0)
    b_l = lax.broadcasted_iota(jnp.int32, (MH, MP), 1)
    sh_h, sh_p = H.bit_length() - 1, P.bit_length() - 1
    own_b = jnp.right_shift(b_r, sh_h) == jnp.right_shift(b_l, sh_p)
    bre = jnp.where(own_b, _place(btre_ref[0], tile_p), 0.0)
    bim = jnp.where(own_b, _place(btim_ref[0], tile_p), 0.0)
    inv = 1.0 / (lr * lr + li * li)
    cr = ((ar_r - 1.0) * lr + ai_r * li) * inv
    ci = (ai_r * lr - (ar_r - 1.0) * li) * inv
    bre, bim = _cmul(cr, ci, bre, bim)

    u_r = lax.broadcasted_iota(jnp.int32, (H, MH), 0)
    u_l = lax.broadcasted_iota(jnp.int32, (H, MH), 1)
    tile_h = (jnp.bitwise_and(u_l, H - 1) == u_r).astype(BF16)
    c_r = lax.broadcasted_iota(jnp.int32, (MP, MH), 0)
    c_l = lax.broadcasted_iota(jnp.int32, (MP, MH), 1)
    own_c = jnp.right_shift(c_r, sh_p) == jnp.right_shift(c_l, sh_h)
    cre = jnp.where(own_c, _place(ctre_ref[0], tile_h), 0.0)
    cim = jnp.where(own_c, _place(ctim_ref[0], tile_h), 0.0)

    qre, qim = cre, cim
    for tau in range(T + 1):
        pre_ref[:, tau * MH:(tau + 1) * MH] = qre
        pim_ref[:, tau * MH:(tau + 1) * MH] = qim
        if tau < T:
            qre, qim = _cmul(ar_c, ai_c, qre, qim)

    bd = _dot_hi(bre, pre_ref[:, 0:T * MH]) - _dot_hi(bim, pim_ref[:, 0:T * MH])
    d_r = lax.broadcasted_iota(jnp.int32, (MH, MH), 0)
    d_l = lax.broadcasted_iota(jnp.int32, (MH, MH), 1)
    bd_ref[0, :, 0:MH] = (bd[:, 0:MH] + jnp.where(d_r == d_l, d_ref[0], 0.0)).astype(BF16)
    bd_ref[0, :, MH:T * MH] = bd[:, MH:T * MH].astype(BF16)

    v_ref[0, 0:MP, :] = pre_ref[:, MH:(T + 1) * MH].astype(BF16)
    v_ref[0, MP:2 * MP, :] = (-pim_ref[:, MH:(T + 1) * MH]).astype(BF16)

    wre, wim = bre, bim
    for t in range(T - 1, -1, -1):
        w_ref[0, t * MH:(t + 1) * MH, 0:MP] = wre.astype(BF16)
        w_ref[0, t * MH:(t + 1) * MH, MP:2 * MP] = wim.astype(BF16)
        if t > 0:
            wre, wim = _cmul(ar_r, ai_r, wre, wim)

    sr, si = ar_r, ai_r
    for _ in range(4):
        sr, si = _cmul(sr, si, sr, si)
    row8 = lax.broadcasted_iota(jnp.int32, (8, MP), 0)
    a16_ref[0] = jnp.where(row8 == 0, sr, si)


def _s5_prep(log_dt, lam_re, lam_im, b_re, b_im, c_re, c_im, d):
    G, P = lam_re.shape
    H, T, M = S5_GROUP, S5_CHUNK, S5_BLOCK
    J = G // M
    MP, MH = M * P, M * H
    ldt = jnp.repeat(log_dt, P)
    args = [
        ldt.reshape(J, 1, MP), lam_re.reshape(J, 1, MP), lam_im.reshape(J, 1, MP),
        ldt.reshape(J, MP, 1), lam_re.reshape(J, MP, 1), lam_im.reshape(J, MP, 1),
        b_re.transpose(0, 2, 1).reshape(J, MH, P), b_im.transpose(0, 2, 1).reshape(J, MH, P),
        c_re.transpose(0, 2, 1).reshape(J, MP, H), c_im.transpose(0, 2, 1).reshape(J, MP, H),
        d.reshape(J, 1, MH),
    ]
    blk = lambda s: pl.BlockSpec((1,) + tuple(s[1:]), lambda j: (j, 0, 0))
    outs = [((J, MH, T * MH), BF16), ((J, T * MH, 2 * MP), BF16), ((J, 2 * MP, T * MH), BF16),
            ((J, 8, MP), F32)]
    return pl.pallas_call(
        _s5_prep_kernel,
        grid=(J,),
        in_specs=[blk(a.shape) for a in args],
        out_specs=[blk(s) for s, _ in outs],
        out_shape=[jax.ShapeDtypeStruct(s, dt) for s, dt in outs],
        scratch_shapes=[pltpu.VMEM((MP, (T + 1) * MH), F32)] * 2,
        compiler_params=_params(("parallel",)),
        name="s5_prep",
    )(*args)


def _s5_scan_kernel(ut_ref, bd_ref, w_ref, v_ref, a16_ref, yt_ref, big_ref, xs_ref, sp_ref):
    T = S5_CHUNK
    R = ut_ref.shape[1]
    MH = ut_ref.shape[2]
    MP = a16_ref.shape[2]

    @pl.when(pl.program_id(1) == 0)
    def _():
        for s in range(T):
            for t in range(T):
                blk = (bd_ref[0, :, (t - s) * MH:(t - s + 1) * MH] if t >= s
                       else jnp.zeros((MH, MH), BF16))
                big_ref[s * MH:(s + 1) * MH, t * MH:(t + 1) * MH] = blk

    x = jnp.concatenate([ut_ref[t] for t in range(T)], axis=1)
    xs_ref[...] = _dot(x, w_ref[0])
    a16 = a16_ref[0]
    a1 = (jnp.broadcast_to(a16[0:1], (8, MP)), jnp.broadcast_to(a16[1:2], (8, MP)))
    a2 = _cmul(*a1, *a1)
    a4 = _cmul(*a2, *a2)
    rid = lax.broadcasted_iota(jnp.int32, (8, MP), 0)
    pw = a1
    for r in range(1, 8):
        nxt = _cmul(*pw, *a1)
        pw = (jnp.where(rid >= r, nxt[0], pw[0]), jnp.where(rid >= r, nxt[1], pw[1]))

    def shift_rows(v, n):
        return jnp.where(rid >= n, pltpu.roll(v, n, 0), 0.0)

    def body(i, carry):
        sr, si = carry
        r0 = pl.multiple_of(i * 8, 8)
        yr = xs_ref[pl.ds(r0, 8), 0:MP]
        yi = xs_ref[pl.ds(r0, 8), MP:2 * MP]
        for n, am in ((1, a1), (2, a2), (4, a4)):
            tr, ti = _cmul(*am, shift_rows(yr, n), shift_rows(yi, n))
            yr, yi = yr + tr, yi + ti
        cr, ci = _cmul(*pw, jnp.broadcast_to(sr, (8, MP)), jnp.broadcast_to(si, (8, MP)))
        er, ei = yr + cr, yi + ci
        sp_ref[pl.ds(r0, 8), 0:MP] = jnp.where(rid == 0, sr, pltpu.roll(er, 1, 0))
        sp_ref[pl.ds(r0, 8), MP:2 * MP] = jnp.where(rid == 0, si, pltpu.roll(ei, 1, 0))
        return er[7:8], ei[7:8]

    zero = jnp.zeros((1, MP), F32)
    lax.fori_loop(0, R // 8, body, (zero, zero), unroll=2)

    sp = sp_ref[...].astype(BF16)
    for p in range(T // 2):
        k = 2 * (p + 1) * MH
        cols = slice(2 * p * MH, 2 * (p + 1) * MH)
        acc = _dot(x[:, 0:k], big_ref[0:k, cols]) + _dot(sp, v_ref[0, :, cols])
        yt_ref[2 * p] = acc[:, 0:MH].astype(BF16)
        yt_ref[2 * p + 1] = acc[:, MH:2 * MH].astype(BF16)


def _s5_scan(ut, bd, w, v, a16, n_batch):
    T, nrow, s5w = ut.shape
    R = nrow // n_batch
    MH = LANES
    MP = a16.shape[2]
    per_j = lambda s: pl.BlockSpec((1,) + tuple(s[1:]), lambda j, b: (j, 0, 0))
    io = pl.BlockSpec((T, R, MH), lambda j, b: (0, b, j))
    return pl.pallas_call(
        _s5_scan_kernel,
        grid=(s5w // MH, n_batch),
        in_specs=[io, per_j(bd.shape), per_j(w.shape), per_j(v.shape), per_j(a16.shape)],
        out_specs=io,
        out_shape=jax.ShapeDtypeStruct(ut.shape, BF16),
        scratch_shapes=[pltpu.VMEM((T * MH, T * MH), BF16),
                        pltpu.VMEM((R, 2 * MP), F32),
                        pltpu.VMEM((R, 2 * MP), F32)],
        compiler_params=_params(("arbitrary", "arbitrary")),
        name="s5_scan",
    )(ut, bd, w, v, a16)


def _mlstm_kernel(qk_ref, v_ref, o_ref, if_ref, cw_ref, cb_ref, bif_ref, ng_ref, y_ref,
                  cv_ref, cn_ref, m_ref):
    Lc = M_CHUNK
    S = qk_ref.shape[0]
    W = v_ref.shape[1]
    Dh = W // M_HEADS
    base = 2 * M_PAD

    @pl.when(pl.program_id(1) == 0)
    def _():
        cv_ref[:, 0:base, :] = jnp.zeros((cv_ref.shape[0], base, LANES), F32)
        cn_ref[...] = jnp.zeros(cn_ref.shape, F32)
        m_ref[...] = jnp.zeros(m_ref.shape, F32)

    @pl.when(pl.program_id(1) > 0)
    def _():
        cv_ref[:, 0:base, :] = cv_ref[:, 2 * S:2 * S + base, :]

    x = qk_ref[...].astype(F32)
    n_slab = x.shape[1] // LANES
    for s in range(n_slab):
        cv_ref[s // 2, pl.ds(base + s % 2, S, stride=2), :] = x[:, s * LANES:(s + 1) * LANES]
    cols = []
    for s in range(n_slab):
        l0 = s * LANES
        acc = cb_ref[:, l0:l0 + LANES]
        for k in range(M_CONV):
            start = base + s % 2 - 2 * (M_CONV - 1 - k)
            acc = acc + cw_ref[k:k + 1, l0:l0 + LANES] * cv_ref[s // 2, pl.ds(start, S, stride=2), :]
        cols.append(acc * _sigmoid(acc))
    qk = jnp.concatenate(cols, axis=1)

    n_ci = S // Lc
    inst = [(ci, h) for ci in range(n_ci) for h in range(M_HEADS)]
    rows_of = lambda b: slice(b * Lc, (b + 1) * Lc)
    stack = lambda parts: jnp.concatenate(parts, axis=0)
    col = lambda parts: stack([jnp.broadcast_to(p, (Lc, 1)) for p in parts])
    wide = lambda parts: stack([jnp.broadcast_to(p, (Lc, p.shape[1])) for p in parts])

    lane = lax.broadcasted_iota(jnp.int32, (Lc, Lc), 1)
    row = lax.broadcasted_iota(jnp.int32, (Lc, Lc), 0)
    causal1 = row >= lane
    tril = causal1.astype(BF16)
    ones_col = (lax.broadcasted_iota(jnp.int32, (Lc, Dh), 1) == 0).astype(BF16)

    gm, bcu, gm_t, bcu_t = [], [], [], []
    for ci in range(n_ci):
        pre = if_ref[ci * Lc:(ci + 1) * Lc, :] + bif_ref[...]
        g = jnp.where(lane < M_HEADS, pre, jnp.where(lane < 2 * M_HEADS, _log_sigmoid(pre), 0.0))
        c = _place_left(tril, g)
        gm.append(g)
        bcu.append(c)
        gm_t.append(g.T)
        bcu_t.append(c.T)

    hs = lambda h: slice(h * Dh, (h + 1) * Dh)
    cs = lambda ci: slice(ci * Lc, (ci + 1) * Lc)
    q = stack([qk[cs(ci), hs(h)] for ci, h in inst])
    k = stack([qk[cs(ci), W + h * Dh:W + (h + 1) * Dh] for ci, h in inst]) * (Dh ** -0.5)
    v1 = [jnp.concatenate([v_ref[cs(ci), hs(h)], ones_col], axis=1) for ci, h in inst]
    ig = stack([gm[ci][:, h:h + 1] for ci, h in inst])
    bc = stack([bcu[ci][:, M_HEADS + h:M_HEADS + h + 1] for ci, h in inst])
    ig_t = wide([gm_t[ci][h:h + 1, :] for ci, h in inst])
    bc_t = wide([bcu_t[ci][M_HEADS + h:M_HEADS + h + 1, :] for ci, h in inst])
    g_c = [bcu[ci][Lc - 1:Lc, M_HEADS + h:M_HEADS + h + 1] for ci, h in inst]
    causal = stack([causal1] * len(inst))

    a_end = col(g_c) - bc + ig
    m_loc = [jnp.max(a_end[rows_of(b)], axis=0, keepdims=True) for b in range(len(inst))]
    m_run = [m_ref[h][0:1, 0:1] for h in range(M_HEADS)]
    m_prev, m_new = [], []
    for b, (ci, h) in enumerate(inst):
        m_prev.append(m_run[h])
        m_run[h] = jnp.maximum(g_c[b] + m_run[h], m_loc[b])
        m_new.append(m_run[h])
    w_col = jnp.exp(a_end - col(m_new))

    inter = bc + col(m_prev)
    dlog = jnp.where(causal, bc - bc_t + ig_t, NEG_BIG)
    m_t = jnp.maximum(inter, jnp.max(dlog, axis=1, keepdims=True))
    qb, kb = q.astype(BF16), k.astype(BF16)
    s_mat = stack([_dot_nt(qb[rows_of(b)], kb[rows_of(b)]) for b in range(len(inst))])
    s_mat = s_mat * jnp.exp(dlog - m_t)
    inter_w = jnp.exp(inter - m_t)
    lhs = jnp.concatenate([s_mat.astype(BF16), (inter_w * q).astype(BF16)], axis=1)

    kw = k * w_col
    upd = [_dot(kw[rows_of(b)].T.astype(BF16), v1[b]) for b in range(len(inst))]
    cn_run = [cn_ref[h] for h in range(M_HEADS)]
    nd = []
    for b, (ci, h) in enumerate(inst):
        rhs = jnp.concatenate([v1[b], cn_run[h].astype(BF16)], axis=0)
        nd.append(_dot(lhs[rows_of(b)], rhs))
        cn_run[h] = jnp.exp(g_c[b] + m_prev[b] - m_new[b]) * cn_run[h] + upd[b]
    nd = stack(nd)
    num = nd[:, :Dh]
    den = nd[:, Dh:Dh + 1]
    hh = num / jnp.maximum(jnp.abs(den), jnp.exp(-m_t))
    hh = stack([o_ref[cs(ci), hs(h)] for ci, h in inst]).astype(F32) * hh
    hh = hh * lax.rsqrt(jnp.mean(hh * hh, axis=-1, keepdims=True) + EPS)
    hh = (hh * wide([ng_ref[:, hs(h)] for ci, h in inst])).astype(BF16)
    for b, (ci, h) in enumerate(inst):
        y_ref[cs(ci), hs(h)] = hh[rows_of(b)]

    for h in range(M_HEADS):
        cn_ref[h] = cn_run[h]
        m_ref[h] = jnp.broadcast_to(m_run[h], m_ref.shape[1:])


def _place_left(onehot, x):
    hi, mid, lo = _split3(x)
    return _dot(onehot, hi) + _dot(onehot, mid) + _dot(onehot, lo)


def _mlstm(qk, v, o, gif, conv_w, conv_b, bif, norm_g, n_batch):
    n, w = v.shape
    rows = M_STEP_CHUNKS * M_CHUNK
    nc = n // n_batch // rows
    dh = w // M_HEADS
    row = lambda c: pl.BlockSpec((rows, c), lambda b, i: (b * nc + i, 0))
    return pl.pallas_call(
        _mlstm_kernel,
        grid=(n_batch, nc),
        in_specs=[row(2 * w), row(w), row(w), row(gif.shape[1]), _full(conv_w.shape),
                  _full(conv_b.shape), _full(bif.shape), _full(norm_g.shape)],
        out_specs=row(w),
        out_shape=jax.ShapeDtypeStruct((n, w), BF16),
        scratch_shapes=[pltpu.VMEM((2 * w // LANES // 2, 2 * (M_PAD + rows), LANES), F32),
                        pltpu.VMEM((M_HEADS, dh, 2 * dh), F32),
                        pltpu.VMEM((M_HEADS, 8, LANES), F32)],
        compiler_params=_params(("arbitrary", "arbitrary")),
        name="mlstm",
    )(qk, v, o, gif, conv_w, conv_b, bif, norm_g)


def _mem_kv_kernel(mem_ref, g_ref, wkv_ref, kv_ref):
    kv_ref[...] = _dot(_rmsnorm(mem_ref[...], g_ref[...]).astype(BF16), wkv_ref[...]).astype(BF16)


def _mem_kv(mem2d, g, wkv):
    n, d = mem2d.shape
    return pl.pallas_call(
        _mem_kv_kernel,
        grid=(1,),
        in_specs=[_full(mem2d.shape), _full(g.shape), _full(wkv.shape)],
        out_specs=_full((n, wkv.shape[1])),
        out_shape=jax.ShapeDtypeStruct((n, wkv.shape[1]), BF16),
        compiler_params=_params(("arbitrary",)),
        name="mem_kv",
    )(mem2d, g, wkv)


def _merge_attn_kernel(x_ref, yt_ref, ym_ref, gt_ref, wglu_ref, bglu_ref, wbs_ref, wbm_ref,
                       wout_ref, xg_ref, wq_ref, kv_ref, wo_ref, x2_ref, ys_ref):
    d = x_ref.shape[1]
    rows = yt_ref.shape[1]
    n_slab = yt_ref.shape[2] // LANES
    for t in range(S5_CHUNK):
        for j in range(n_slab):
            ys_ref[j, pl.ds(t, rows, stride=S5_CHUNK), :] = (
                yt_ref[t, :, j * LANES:(j + 1) * LANES].astype(F32))
    ys = _gelu(jnp.concatenate([ys_ref[j] for j in range(n_slab)], axis=1))
    ys = ys * _sigmoid(_dot(ys.astype(BF16), wglu_ref[...]) + bglu_ref[...])
    p_s = _dot(ys.astype(BF16), wbs_ref[...])
    p_m = _dot(ym_ref[...], wbm_ref[...])
    merged = gt_ref[:, :d].astype(F32) * p_s + gt_ref[:, d:].astype(F32) * p_m
    x1 = x_ref[...] + _dot(merged.astype(BF16), wout_ref[...])

    q = _dot(_rmsnorm(x1, xg_ref[...]).astype(BF16), wq_ref[...]).astype(BF16)
    dh = d // X_HEADS
    outs = []
    for h in range(X_HEADS):
        kh = kv_ref[0, :, h * dh:(h + 1) * dh]
        vh = kv_ref[0, :, d + h * dh:d + (h + 1) * dh]
        s = _dot_nt(q[:, h * dh:(h + 1) * dh], kh) * (dh ** -0.5)
        s = s - jnp.max(s, axis=-1, keepdims=True)
        p = jnp.exp(s)
        p = p / jnp.sum(p, axis=-1, keepdims=True)
        outs.append(_dot(p.astype(BF16), vh).astype(BF16))
    o = jnp.concatenate(outs, axis=1)
    x2_ref[...] = x1 + _dot(o, wo_ref[...])


def _merge_attn(x2d, yt, ym, gt, wglu, bglu, wbs, wbm, wout, xg, wq, kv, wo, n_batch):
    n, d = x2d.shape
    tm = TOKEN_TILE
    per_b = n // n_batch // tm
    row = lambda c: pl.BlockSpec((tm, c), lambda b, i: (b * per_b + i, 0))
    n_mem = kv.shape[1]
    T, _, s5w = yt.shape
    return pl.pallas_call(
        _merge_attn_kernel,
        grid=(n_batch, per_b),
        in_specs=[row(d), pl.BlockSpec((T, tm // T, s5w), lambda b, i: (0, b * per_b + i, 0)),
                  row(ym.shape[1]), row(gt.shape[1]),
                  _full(wglu.shape), _full(bglu.shape), _full(wbs.shape), _full(wbm.shape),
                  _full(wout.shape), _full(xg.shape), _full(wq.shape),
                  pl.BlockSpec((1, n_mem, kv.shape[2]), lambda b, i: (b, 0, 0)),
                  _full(wo.shape)],
        out_specs=row(d),
        out_shape=jax.ShapeDtypeStruct((n, d), F32),
        scratch_shapes=[pltpu.VMEM((s5w // LANES, tm, LANES), F32)],
        compiler_params=_params(("parallel", "parallel")),
        name="merge_attn",
    )(x2d, yt, ym, gt, wglu, bglu, wbs, wbm, wout, xg, wq, kv, wo)


def _conv_ffn_kernel(x_ref, fg_ref, wup_ref, cw_ref, cb_ref, wdn_ref, ng_ref, o_ref,
                     up_ref, *, final_norm):
    tm = x_ref.shape[0]
    dff = wdn_ref.shape[0]
    ck = FFN_CHUNK
    n_chunks = dff // ck
    spc = ck // LANES
    base = 2 * FFN_PAD

    @pl.when(pl.program_id(1) == 0)
    def _():
        up_ref[:, 0:base, :] = jnp.zeros((up_ref.shape[0], base, LANES), F32)

    @pl.when(pl.program_id(1) > 0)
    def _():
        up_ref[:, 0:base, :] = up_ref[:, 2 * tm:2 * tm + base, :]

    x2 = x_ref[...]
    hb = _rmsnorm(x2, fg_ref[...]).astype(BF16)

    def up(c, half):
        c0 = half * dff + c * ck
        u = _dot(hb, wup_ref[:, c0:c0 + ck])
        for s in range(spc):
            up_ref[c * spc + s, pl.ds(base + half, tm, stride=2), :] = u[:, s * LANES:(s + 1) * LANES]

    def conv(c, half):
        c0 = half * dff + c * ck
        cols = []
        for s in range(spc):
            l0 = c0 + s * LANES
            acc = cb_ref[:, l0:l0 + LANES]
            for k in range(FFN_CONV):
                start = base + half - 2 * (FFN_CONV - 1 - k)
                acc = acc + (cw_ref[k:k + 1, l0:l0 + LANES]
                             * up_ref[c * spc + s, pl.ds(start, tm, stride=2), :])
            cols.append(acc)
        return jnp.concatenate(cols, axis=1)

    y = x2
    up(0, 0)
    up(0, 1)
    for c in range(n_chunks):
        if c + 1 < n_chunks:
            up(c + 1, 0)
            up(c + 1, 1)
        a = conv(c, 0)
        ab = a * conv(c, 1)
        th = jnp.tanh(a * (GELU_C1 + GELU_C3 * (a * a)))
        act = (ab + ab * th).astype(BF16)
        y = y + _dot(act, wdn_ref[c * ck:(c + 1) * ck, :])
    o_ref[...] = _rmsnorm(y, ng_ref[...]) if final_norm else y


def _conv_ffn(x2, fg, wup, cw, cb, wdn, ng, n_batch, final_norm):
    n, d = x2.shape
    tm = TOKEN_TILE
    per_b = n // n_batch // tm
    row = pl.BlockSpec((tm, d), lambda b, i: (b * per_b + i, 0))
    return pl.pallas_call(
        functools.partial(_conv_ffn_kernel, final_norm=final_norm),
        grid=(n_batch, per_b),
        in_specs=[row, _full(fg.shape), _full(wup.shape), _full(cw.shape), _full(cb.shape),
                  _full(wdn.shape), _full(ng.shape)],
        out_specs=row,
        out_shape=jax.ShapeDtypeStruct((n, d), F32),
        scratch_shapes=[pltpu.VMEM((wdn.shape[0] // LANES, 2 * (FFN_PAD + tm), LANES), F32)],
        compiler_params=_params(("arbitrary", "arbitrary")),
        name="conv_ffn",
    )(x2, fg, wup, cw, cb, wdn, ng)


def kernel(x, mem, mix_norm_g, w_in, s5_lam_re, s5_lam_im, s5_b_re, s5_b_im, s5_c_re, s5_c_im,
           s5_d, s5_log_dt, s5_w_glu, s5_b_glu, m_conv_w, m_conv_b, m_b_i, m_b_f, m_norm_g,
           w_br_s5, w_br_m, b_gate, w_out, x_norm_g, mem_norm_g, x_wq, x_wkv, x_wo,
           f_norm_g, f_w_up, f_conv_w, f_conv_b, f_w_down, final_norm_g):
    B, L, D = x.shape
    N = B * L
    depth = w_in.shape[0]
    G = s5_lam_re.shape[1]
    s5w = G * S5_GROUP
    mw = (w_in.shape[2] - s5w - 2 * M_HEADS - 2 * D) // 4
    row = lambda a: a.reshape(1, -1)
    bf = lambda a: a.astype(BF16)

    xs = x.reshape(N, D)
    mem2d = mem.reshape(-1, D)
    for l in range(depth):
        wl = w_in[l]
        c0 = s5w
        c1 = c0 + 2 * mw
        c2 = c1 + mw
        c3 = c2 + mw
        c4 = c3 + 2 * M_HEADS
        wif = jnp.pad(wl[:, c3:c4], ((0, 0), (0, LANES - 2 * M_HEADS)))
        ut, qk, v, o, gif, gates = _in_proj(
            xs, row(mix_norm_g[l]), bf(wl[:, :c0]), bf(wl[:, c0:c1]), bf(wl[:, c1:c2]),
            bf(wl[:, c2:c3]), bf(wif), bf(wl[:, c4:]), row(b_gate[l]))

        bd, w_st, v_st, a16 = _s5_prep(
            s5_log_dt[l], s5_lam_re[l], s5_lam_im[l], s5_b_re[l], s5_b_im[l], s5_c_re[l],
            s5_c_im[l], s5_d[l])
        yt = _s5_scan(ut, bd, w_st, v_st, a16, B)

        bif = jnp.pad(jnp.concatenate([m_b_i[l], m_b_f[l]]), (0, LANES - 2 * M_HEADS)).reshape(1, LANES)
        y_m = _mlstm(qk, v, o, gif, m_conv_w[l], row(m_conv_b[l]), bif, row(m_norm_g[l]), B)

        kv = _mem_kv(mem2d, row(mem_norm_g[l]), bf(x_wkv[l])).reshape(B, -1, 2 * D)
        x2 = _merge_attn(xs, yt, y_m, gates, bf(s5_w_glu[l]), row(s5_b_glu[l]), bf(w_br_s5[l]),
                         bf(w_br_m[l]), bf(w_out[l]), row(x_norm_g[l]), bf(x_wq[l]), kv,
                         bf(x_wo[l]), B)
        dff = f_w_down.shape[1]
        half = jnp.concatenate([jnp.ones((dff,), F32), jnp.full((dff,), 0.5, F32)])
        xs = _conv_ffn(x2, row(f_norm_g[l]), bf(f_w_up[l]), f_conv_w[l] * half,
                       row(f_conv_b[l] * half), bf(f_w_down[l]), row(final_norm_g), B,
                       l == depth - 1)
    return xs.reshape(B, L, D)
```

```python
import functools
import math

import jax
import jax.numpy as jnp
from jax import lax
from jax.experimental import pallas as pl
from jax.experimental.pallas import tpu as pltpu

F32 = jnp.float32
BF16 = jnp.bfloat16

EPS = 1e-6
S5_GROUP = 16
S5_STATE = 64
S5_CHUNK = 16
M_HEADS = 4
M_CONV = 4
M_CHUNK = 128
M_STEP_CHUNKS = 8
M_PAD = 8
X_HEADS = 4
FFN_CONV = 3
NEG_BIG = -1e30

LANES = 128
S5_BLOCK = LANES // S5_GROUP
TOKEN_TILE = 512
PROJ_TILE = 1024
FFN_CHUNK = 256
FFN_PAD = 8
VMEM_LIMIT = 56 * 1024 * 1024


def _dot(a, b):
    return jnp.dot(a, b, preferred_element_type=F32)


def _dot_nt(a, b):
    return lax.dot_general(a, b, (((1,), (1,)), ((), ())), preferred_element_type=F32)


def _sigmoid(x):
    return 1.0 / (1.0 + jnp.exp(-x))


def _log_sigmoid(x):
    return jnp.minimum(x, 0.0) - jnp.log(1.0 + jnp.exp(-jnp.abs(x)))


GELU_C1 = math.sqrt(2.0 / math.pi)
GELU_C3 = 0.044715 * GELU_C1


def _gelu(x):
    return x * (0.5 * (1.0 + jnp.tanh(x * (GELU_C1 + GELU_C3 * (x * x)))))


def _rmsnorm(x, g):
    return x * lax.rsqrt(jnp.mean(x * x, axis=-1, keepdims=True) + EPS) * g


def _cmul(ar, ai, br, bi):
    return ar * br - ai * bi, ar * bi + ai * br


def _full(shape):
    n = len(shape)
    return pl.BlockSpec(shape, lambda *_: (0,) * n)


def _params(sem):
    return pltpu.CompilerParams(dimension_semantics=sem, vmem_limit_bytes=VMEM_LIMIT)


def _in_proj_kernel(x_ref, g_ref, wu_ref, wqk_ref, wv_ref, wo_ref, wif_ref, wg_ref, bg_ref,
                    ut_ref, qk_ref, v_ref, o_ref, if_ref, gt_ref, us_ref):
    hb = _rmsnorm(x_ref[...], g_ref[...]).astype(BF16)
    u = _dot(hb, wu_ref[...])
    n_slab = u.shape[1] // LANES
    for j in range(n_slab):
        us_ref[j] = u[:, j * LANES:(j + 1) * LANES]
    rows = u.shape[0] // S5_CHUNK
    for t in range(S5_CHUNK):
        for j in range(n_slab):
            ut_ref[t, :, j * LANES:(j + 1) * LANES] = (
                us_ref[j, pl.ds(t, rows, stride=S5_CHUNK), :].astype(BF16))
    qk_ref[...] = _dot(hb, wqk_ref[...]).astype(BF16)
    v_ref[...] = _dot(hb, wv_ref[...]).astype(BF16)
    o_ref[...] = _sigmoid(_dot(hb, wo_ref[...])).astype(BF16)
    if_ref[...] = _dot(hb, wif_ref[...])
    gt_ref[...] = _sigmoid(_dot(hb, wg_ref[...]) + bg_ref[...]).astype(BF16)


def _in_proj(x2d, g, wu, wqk, wv, wo, wif, wg, bg):
    n, d = x2d.shape
    tm = PROJ_TILE
    row = lambda c: pl.BlockSpec((tm, c), lambda i: (i, 0))
    outs = [(wqk.shape[1], BF16), (wv.shape[1], BF16), (wo.shape[1], BF16),
            (wif.shape[1], F32), (wg.shape[1], BF16)]
    s5w = wu.shape[1]
    T = S5_CHUNK
    return pl.pallas_call(
        _in_proj_kernel,
        grid=(n // tm,),
        in_specs=[row(d), _full(g.shape), _full(wu.shape), _full(wqk.shape), _full(wv.shape),
                  _full(wo.shape), _full(wif.shape), _full(wg.shape), _full(bg.shape)],
        out_specs=[pl.BlockSpec((T, tm // T, s5w), lambda i: (0, i, 0))] + [row(c) for c, _ in outs],
        out_shape=([jax.ShapeDtypeStruct((T, n // T, s5w), BF16)]
                   + [jax.ShapeDtypeStruct((n, c), dt) for c, dt in outs]),
        scratch_shapes=[pltpu.VMEM((s5w // LANES, tm, LANES), F32)],
        compiler_params=_params(("parallel",)),
        name="in_proj",
    )(x2d, g, wu, wqk, wv, wo, wif, wg, bg)


def _split3(x):
    hi = x.astype(BF16)
    r1 = x - hi.astype(F32)
    mid = r1.astype(BF16)
    lo = (r1 - mid.astype(F32)).astype(BF16)
    return hi, mid, lo


def _dot_x3(a, b):
    a_hi, a_lo, _ = _split3(a)
    b_hi, b_lo, _ = _split3(b)
    return _dot(a_hi, b_hi) + _dot(a_hi, b_lo) + _dot(a_lo, b_hi)


def _place(x, onehot):
    hi, mid, lo = _split3(x)
    return _dot(hi, onehot) + _dot(mid, onehot) + _dot(lo, onehot)


def _s5_prep_kernel(ldt_r_ref, lre_r_ref, lim_r_ref, ldt_c_ref, lre_c_ref, lim_c_ref,
                    btre_ref, btim_ref, ctre_ref, ctim_ref, d_ref,
                    bd_ref, w_ref, v_ref, a16_ref, pre_ref, pim_ref):
    P, H, T, M = S5_STATE, S5_GROUP, S5_CHUNK, S5_BLOCK
    MP, MH = M * P, M * H

    lr, li = lre_r_ref[0], lim_r_ref[0]
    dt_r = jnp.exp(ldt_r_ref[0])
    e_r = jnp.exp(lr * dt_r)
    ar_r, ai_r = e_r * jnp.cos(li * dt_r), e_r * jnp.sin(li * dt_r)
    dt_c = jnp.exp(ldt_c_ref[0])
    e_c = jnp.exp(lre_c_ref[0] * dt_c)
    ar_c, ai_c = e_c * jnp.cos(lim_c_ref[0] * dt_c), e_c * jnp.sin(lim_c_ref[0] * dt_c)

    t_r = lax.broadcasted_iota(jnp.int32, (P, MP), 0)
    t_l = lax.broadcasted_iota(jnp.int32, (P, MP), 1)
    tile_p = (jnp.bitwise_and(t_l, P - 1) == t_r).astype(BF16)
    b_r = lax.broadcasted_iota(jnp.int32, (MH, MP), 0)
    b_l = lax.broadcasted_iota(jnp.int32, (MH, MP), 1)
    sh_h, sh_p = H.bit_length() - 1, P.bit_length() - 1
    own_b = jnp.right_shift(b_r, sh_h) == jnp.right_shift(b_l, sh_p)
    bre = jnp.where(own_b, _place(btre_ref[0], tile_p), 0.0)
    bim = jnp.where(own_b, _place(btim_ref[0], tile_p), 0.0)
    inv = 1.0 / (lr * lr + li * li)
    cr = ((ar_r - 1.0) * lr + ai_r * li) * inv
    ci = (ai_r * lr - (ar_r - 1.0) * li) * inv
    bre, bim = _cmul(cr, ci, bre, bim)

    u_r = lax.broadcasted_iota(jnp.int32, (H, MH), 0)
    u_l = lax.broadcasted_iota(jnp.int32, (H, MH), 1)
    tile_h = (jnp.bitwise_and(u_l, H - 1) == u_r).astype(BF16)
    c_r = lax.broadcasted_iota(jnp.int32, (MP, MH), 0)
    c_l = lax.broadcasted_iota(jnp.int32, (MP, MH), 1)
    own_c = jnp.right_shift(c_r, sh_p) == jnp.right_shift(c_l, sh_h)
    cre = jnp.where(own_c, _place(ctre_ref[0], tile_h), 0.0)
    cim = jnp.where(own_c, _place(ctim_ref[0], tile_h), 0.0)

    qre, qim = cre, cim
    for tau in range(T + 1):
        pre_ref[:, tau * MH:(tau + 1) * MH] = qre
        pim_ref[:, tau * MH:(tau + 1) * MH] = qim
        if tau < T:
            qre, qim = _cmul(ar_c, ai_c, qre, qim)

    bd = _dot_x3(bre, pre_ref[:, 0:T * MH]) - _dot_x3(bim, pim_ref[:, 0:T * MH])
    d_r = lax.broadcasted_iota(jnp.int32, (MH, MH), 0)
    d_l = lax.broadcasted_iota(jnp.int32, (MH, MH), 1)
    bd_ref[0, :, 0:MH] = (bd[:, 0:MH] + jnp.where(d_r == d_l, d_ref[0], 0.0)).astype(BF16)
    bd_ref[0, :, MH:T * MH] = bd[:, MH:T * MH].astype(BF16)

    v_ref[0, 0:MP, :] = pre_ref[:, MH:(T + 1) * MH].astype(BF16)
    v_ref[0, MP:2 * MP, :] = (-pim_ref[:, MH:(T + 1) * MH]).astype(BF16)

    wre, wim = bre, bim
    for t in range(T - 1, -1, -1):
        w_ref[0, t * MH:(t + 1) * MH, 0:MP] = wre.astype(BF16)
        w_ref[0, t * MH:(t + 1) * MH, MP:2 * MP] = wim.astype(BF16)
        if t > 0:
            wre, wim = _cmul(ar_r, ai_r, wre, wim)

    sr, si = ar_r, ai_r
    for _ in range(4):
        sr, si = _cmul(sr, si, sr, si)
    row8 = lax.broadcasted_iota(jnp.int32, (8, MP), 0)
    a16_ref[0] = jnp.where(row8 == 0, sr, si)


def _s5_prep(log_dt, lam_re, lam_im, b_re, b_im, c_re, c_im, d):
    G, P = lam_re.shape
    H, T, M = S5_GROUP, S5_CHUNK, S5_BLOCK
    J = G // M
    MP, MH = M * P, M * H
    ldt = jnp.repeat(log_dt, P)
    args = [
        ldt.reshape(J, 1, MP), lam_re.reshape(J, 1, MP), lam_im.reshape(J, 1, MP),
        ldt.reshape(J, MP, 1), lam_re.reshape(J, MP, 1), lam_im.reshape(J, MP, 1),
        b_re.transpose(0, 2, 1).reshape(J, MH, P), b_im.transpose(0, 2, 1).reshape(J, MH, P),
        c_re.transpose(0, 2, 1).reshape(J, MP, H), c_im.transpose(0, 2, 1).reshape(J, MP, H),
        d.reshape(J, 1, MH),
    ]
    blk = lambda s: pl.BlockSpec((1,) + tuple(s[1:]), lambda j: (j, 0, 0))
    outs = [((J, MH, T * MH), BF16), ((J, T * MH, 2 * MP), BF16), ((J, 2 * MP, T * MH), BF16),
            ((J, 8, MP), F32)]
    return pl.pallas_call(
        _s5_prep_kernel,
        grid=(J,),
        in_specs=[blk(a.shape) for a in args],
        out_specs=[blk(s) for s, _ in outs],
        out_shape=[jax.ShapeDtypeStruct(s, dt) for s, dt in outs],
        scratch_shapes=[pltpu.VMEM((MP, (T + 1) * MH), F32)] * 2,
        compiler_params=_params(("parallel",)),
        name="s5_prep",
    )(*args)


def _s5_scan_kernel(ut_ref, bd_ref, w_ref, v_ref, a16_ref, yt_ref, big_ref, xs_ref, sp_ref):
    T = S5_CHUNK
    R = ut_ref.shape[1]
    MH = ut_ref.shape[2]
    MP = a16_ref.shape[2]

    @pl.when(pl.program_id(1) == 0)
    def _():
        for s in range(T):
            for t in range(T):
                blk = (bd_ref[0, :, (t - s) * MH:(t - s + 1) * MH] if t >= s
                       else jnp.zeros((MH, MH), BF16))
                big_ref[s * MH:(s + 1) * MH, t * MH:(t + 1) * MH] = blk

    x = jnp.concatenate([ut_ref[t] for t in range(T)], axis=1)
    xs_ref[...] = _dot(x, w_ref[0])
    a16 = a16_ref[0]
    a1 = (jnp.broadcast_to(a16[0:1], (8, MP)), jnp.broadcast_to(a16[1:2], (8, MP)))
    a2 = _cmul(*a1, *a1)
    a4 = _cmul(*a2, *a2)
    rid = lax.broadcasted_iota(jnp.int32, (8, MP), 0)
    pw = a1
    for r in range(1, 8):
        nxt = _cmul(*pw, *a1)
        pw = (jnp.where(rid >= r, nxt[0], pw[0]), jnp.where(rid >= r, nxt[1], pw[1]))

    def shift_rows(v, n):
        return jnp.where(rid >= n, pltpu.roll(v, n, 0), 0.0)

    def body(i, carry):
        sr, si = carry
        r0 = pl.multiple_of(i * 8, 8)
        yr = xs_ref[pl.ds(r0, 8), 0:MP]
        yi = xs_ref[pl.ds(r0, 8), MP:2 * MP]
        for n, am in ((1, a1), (2, a2), (4, a4)):
            tr, ti = _cmul(*am, shift_rows(yr, n), shift_rows(yi, n))
            yr, yi = yr + tr, yi + ti
        cr, ci = _cmul(*pw, jnp.broadcast_to(sr, (8, MP)), jnp.broadcast_to(si, (8, MP)))
        er, ei = yr + cr, yi + ci
        sp_ref[pl.ds(r0, 8), 0:MP] = jnp.where(rid == 0, sr, pltpu.roll(er, 1, 0))
        sp_ref[pl.ds(r0, 8), MP:2 * MP] = jnp.where(rid == 0, si, pltpu.roll(ei, 1, 0))
        return er[7:8], ei[7:8]

    zero = jnp.zeros((1, MP), F32)
    lax.fori_loop(0, R // 8, body, (zero, zero), unroll=2)

    sp = sp_ref[...].astype(BF16)
    for p in range(T // 2):
        k = 2 * (p + 1) * MH
        cols = slice(2 * p * MH, 2 * (p + 1) * MH)
        acc = _dot(x[:, 0:k], big_ref[0:k, cols]) + _dot(sp, v_ref[0, :, cols])
        yt_ref[2 * p] = acc[:, 0:MH].astype(BF16)
        yt_ref[2 * p + 1] = acc[:, MH:2 * MH].astype(BF16)


def _s5_scan(ut, bd, w, v, a16, n_batch):
    T, nrow, s5w = ut.shape
    R = nrow // n_batch
    MH = LANES
    MP = a16.shape[2]
    per_j = lambda s: pl.BlockSpec((1,) + tuple(s[1:]), lambda j, b: (j, 0, 0))
    io = pl.BlockSpec((T, R, MH), lambda j, b: (0, b, j))
    return pl.pallas_call(
        _s5_scan_kernel,
        grid=(s5w // MH, n_batch),
        in_specs=[io, per_j(bd.shape), per_j(w.shape), per_j(v.shape), per_j(a16.shape)],
        out_specs=io,
        out_shape=jax.ShapeDtypeStruct(ut.shape, BF16),
        scratch_shapes=[pltpu.VMEM((T * MH, T * MH), BF16),
                        pltpu.VMEM((R, 2 * MP), F32),
                        pltpu.VMEM((R, 2 * MP), F32)],
        compiler_params=_params(("arbitrary", "arbitrary")),
        name="s5_scan",
    )(ut, bd, w, v, a16)


def _mlstm_kernel(qk_ref, v_ref, o_ref, if_ref, cw_ref, cb_ref, bif_ref, ng_ref, y_ref,
                  cv_ref, cn_ref, m_ref):
    Lc = M_CHUNK
    S = qk_ref.shape[0]
    W = v_ref.shape[1]
    Dh = W // M_HEADS
    base = 2 * M_PAD

    @pl.when(pl.program_id(1) == 0)
    def _():
        cv_ref[:, 0:base, :] = jnp.zeros((cv_ref.shape[0], base, LANES), F32)
        cn_ref[...] = jnp.zeros(cn_ref.shape, F32)
        m_ref[...] = jnp.zeros(m_ref.shape, F32)

    @pl.when(pl.program_id(1) > 0)
    def _():
        cv_ref[:, 0:base, :] = cv_ref[:, 2 * S:2 * S + base, :]

    x = qk_ref[...].astype(F32)
    n_slab = x.shape[1] // LANES
    for s in range(n_slab):
        cv_ref[s // 2, pl.ds(base + s % 2, S, stride=2), :] = x[:, s * LANES:(s + 1) * LANES]
    cols = []
    for s in range(n_slab):
        l0 = s * LANES
        acc = cb_ref[:, l0:l0 + LANES]
        for k in range(M_CONV):
            start = base + s % 2 - 2 * (M_CONV - 1 - k)
            acc = acc + cw_ref[k:k + 1, l0:l0 + LANES] * cv_ref[s // 2, pl.ds(start, S, stride=2), :]
        cols.append(acc * _sigmoid(acc))
    qk = jnp.concatenate(cols, axis=1)

    n_ci = S // Lc
    inst = [(ci, h) for ci in range(n_ci) for h in range(M_HEADS)]
    rows_of = lambda b: slice(b * Lc, (b + 1) * Lc)
    stack = lambda parts: jnp.concatenate(parts, axis=0)
    col = lambda parts: stack([jnp.broadcast_to(p, (Lc, 1)) for p in parts])

    lane = lax.broadcasted_iota(jnp.int32, (Lc, Lc), 1)
    row = lax.broadcasted_iota(jnp.int32, (Lc, Lc), 0)
    causal1 = row >= lane
    tril = causal1.astype(BF16)
    ones_col = (lax.broadcasted_iota(jnp.int32, (Lc, Dh), 1) == 0).astype(BF16)

    gm, bcu, gm_t, bcu_t = [], [], [], []
    for ci in range(n_ci):
        pre = if_ref[ci * Lc:(ci + 1) * Lc, :] + bif_ref[...]
        g = jnp.where(lane < M_HEADS, pre, jnp.where(lane < 2 * M_HEADS, _log_sigmoid(pre), 0.0))
        c = _place_left(tril, g)
        gm.append(g)
        bcu.append(c)
        gm_t.append(g.T)
        bcu_t.append(c.T)

    hs = lambda h: slice(h * Dh, (h + 1) * Dh)
    cs = lambda ci: slice(ci * Lc, (ci + 1) * Lc)
    q = stack([qk[cs(ci), hs(h)] for ci, h in inst])
    k = stack([qk[cs(ci), W + h * Dh:W + (h + 1) * Dh] for ci, h in inst]) * (Dh ** -0.5)
    v1 = [jnp.concatenate([v_ref[cs(ci), hs(h)], ones_col], axis=1) for ci, h in inst]
    ig = stack([gm[ci][:, h:h + 1] for ci, h in inst])
    bc = stack([bcu[ci][:, M_HEADS + h:M_HEADS + h + 1] for ci, h in inst])
    g_c = [bcu[ci][Lc - 1:Lc, M_HEADS + h:M_HEADS + h + 1] for ci, h in inst]
    r_mat = stack([jnp.where(causal1, gm_t[ci][h:h + 1, :] - bcu_t[ci][M_HEADS + h:M_HEADS + h + 1, :],
                             NEG_BIG) for ci, h in inst])

    a_end = col(g_c) - bc + ig
    m_loc = [jnp.max(a_end[rows_of(b)], axis=0, keepdims=True) for b in range(len(inst))]
    m_run = [m_ref[h][0:1, 0:1] for h in range(M_HEADS)]
    m_prev, m_new = [], []
    for b, (ci, h) in enumerate(inst):
        m_prev.append(m_run[h])
        m_run[h] = jnp.maximum(g_c[b] + m_run[h], m_loc[b])
        m_new.append(m_run[h])
    w_col = jnp.exp(a_end - col(m_new))

    mp_col = col(m_prev)
    mx = jnp.maximum(mp_col, jnp.max(r_mat, axis=1, keepdims=True))
    m_t = bc + mx
    qb, kb = q.astype(BF16), k.astype(BF16)
    s_mat = stack([_dot_nt(qb[rows_of(b)], kb[rows_of(b)]) for b in range(len(inst))])
    s_mat = s_mat * jnp.exp(r_mat - mx)
    inter_w = jnp.exp(mp_col - mx)
    lhs = jnp.concatenate([s_mat.astype(BF16), (inter_w * q).astype(BF16)], axis=1)

    kw = k * w_col
    upd = [_dot(kw[rows_of(b)].T.astype(BF16), v1[b]) for b in range(len(inst))]
    cn_run = [cn_ref[h] for h in range(M_HEADS)]
    nd = []
    for b, (ci, h) in enumerate(inst):
        rhs = jnp.concatenate([v1[b], cn_run[h].astype(BF16)], axis=0)
        nd.append(_dot(lhs[rows_of(b)], rhs))
        cn_run[h] = jnp.exp(g_c[b] + m_prev[b] - m_new[b]) * cn_run[h] + upd[b]
    nd = stack(nd)
    num = nd[:, :Dh]
    den = nd[:, Dh:Dh + 1]
    r = 1.0 / jnp.maximum(jnp.abs(den), jnp.exp(-m_t))
    z = stack([o_ref[cs(ci), hs(h)] for ci, h in inst]).astype(F32) * num
    scale = r * lax.rsqrt(r * r * jnp.mean(z * z, axis=-1, keepdims=True) + EPS)
    hh = z * scale
    for b, (ci, h) in enumerate(inst):
        y_ref[cs(ci), hs(h)] = (hh[rows_of(b)] * ng_ref[:, hs(h)]).astype(BF16)

    for h in range(M_HEADS):
        cn_ref[h] = cn_run[h]
        m_ref[h] = jnp.broadcast_to(m_run[h], m_ref.shape[1:])


def _place_left(onehot, x):
    hi, mid, lo = _split3(x)
    return _dot(onehot, hi) + _dot(onehot, mid) + _dot(onehot, lo)


def _mlstm(qk, v, o, gif, conv_w, conv_b, bif, norm_g, n_batch):
    n, w = v.shape
    rows = M_STEP_CHUNKS * M_CHUNK
    nc = n // n_batch // rows
    dh = w // M_HEADS
    row = lambda c: pl.BlockSpec((rows, c), lambda b, i: (b * nc + i, 0))
    return pl.pallas_call(
        _mlstm_kernel,
        grid=(n_batch, nc),
        in_specs=[row(2 * w), row(w), row(w), row(gif.shape[1]), _full(conv_w.shape),
                  _full(conv_b.shape), _full(bif.shape), _full(norm_g.shape)],
        out_specs=row(w),
        out_shape=jax.ShapeDtypeStruct((n, w), BF16),
        scratch_shapes=[pltpu.VMEM((2 * w // LANES // 2, 2 * (M_PAD + rows), LANES), F32),
                        pltpu.VMEM((M_HEADS, dh, 2 * dh), F32),
                        pltpu.VMEM((M_HEADS, 8, LANES), F32)],
        compiler_params=_params(("arbitrary", "arbitrary")),
        name="mlstm",
    )(qk, v, o, gif, conv_w, conv_b, bif, norm_g)


def _mem_kv_kernel(mem_ref, g_ref, wkv_ref, kv_ref):
    kv_ref[...] = _dot(_rmsnorm(mem_ref[...], g_ref[...]).astype(BF16), wkv_ref[...]).astype(BF16)


def _mem_kv(mem2d, g, wkv):
    n, d = mem2d.shape
    return pl.pallas_call(
        _mem_kv_kernel,
        grid=(1,),
        in_specs=[_full(mem2d.shape), _full(g.shape), _full(wkv.shape)],
        out_specs=_full((n, wkv.shape[1])),
        out_shape=jax.ShapeDtypeStruct((n, wkv.shape[1]), BF16),
        compiler_params=_params(("arbitrary",)),
        name="mem_kv",
    )(mem2d, g, wkv)


def _merge_attn_kernel(x_ref, yt_ref, ym_ref, gt_ref, wglu_ref, bglu_ref, wbs_ref, wbm_ref,
                       wout_ref, xg_ref, wq_ref, kv_ref, wo_ref, x2_ref, ys_ref):
    d = x_ref.shape[1]
    rows = yt_ref.shape[1]
    n_slab = yt_ref.shape[2] // LANES
    p_m = _dot(ym_ref[...], wbm_ref[...])
    for t in range(S5_CHUNK):
        for j in range(n_slab):
            ys_ref[j, pl.ds(t, rows, stride=S5_CHUNK), :] = (
                yt_ref[t, :, j * LANES:(j + 1) * LANES].astype(F32))
    ys = _gelu(jnp.concatenate([ys_ref[j] for j in range(n_slab)], axis=1))
    ys = ys * _sigmoid(_dot(ys.astype(BF16), wglu_ref[...]) + bglu_ref[...])
    p_s = _dot(ys.astype(BF16), wbs_ref[...])
    merged = gt_ref[:, :d].astype(F32) * p_s + gt_ref[:, d:].astype(F32) * p_m
    x1 = x_ref[...] + _dot(merged.astype(BF16), wout_ref[...])

    q = _dot(_rmsnorm(x1, xg_ref[...]).astype(BF16), wq_ref[...]).astype(BF16)
    dh = d // X_HEADS
    outs = []
    for h in range(X_HEADS):
        kh = kv_ref[0, :, h * dh:(h + 1) * dh]
        vh = kv_ref[0, :, d + h * dh:d + (h + 1) * dh]
        s = _dot_nt(q[:, h * dh:(h + 1) * dh], kh) * (dh ** -0.5)
        s = s - jnp.max(s, axis=-1, keepdims=True)
        p = jnp.exp(s)
        p = p / jnp.sum(p, axis=-1, keepdims=True)
        outs.append(_dot(p.astype(BF16), vh).astype(BF16))
    o = jnp.concatenate(outs, axis=1)
    x2_ref[...] = x1 + _dot(o, wo_ref[...])


def _merge_attn(x2d, yt, ym, gt, wglu, bglu, wbs, wbm, wout, xg, wq, kv, wo, n_batch):
    n, d = x2d.shape
    tm = PROJ_TILE
    per_b = n // n_batch // tm
    row = lambda c: pl.BlockSpec((tm, c), lambda b, i: (b * per_b + i, 0))
    n_mem = kv.shape[1]
    T, _, s5w = yt.shape
    return pl.pallas_call(
        _merge_attn_kernel,
        grid=(n_batch, per_b),
        in_specs=[row(d), pl.BlockSpec((T, tm // T, s5w), lambda b, i: (0, b * per_b + i, 0)),
                  row(ym.shape[1]), row(gt.shape[1]),
                  _full(wglu.shape), _full(bglu.shape), _full(wbs.shape), _full(wbm.shape),
                  _full(wout.shape), _full(xg.shape), _full(wq.shape),
                  pl.BlockSpec((1, n_mem, kv.shape[2]), lambda b, i: (b, 0, 0)),
                  _full(wo.shape)],
        out_specs=row(d),
        out_shape=jax.ShapeDtypeStruct((n, d), F32),
        scratch_shapes=[pltpu.VMEM((s5w // LANES, tm, LANES), F32)],
        compiler_params=_params(("parallel", "parallel")),
        name="merge_attn",
    )(x2d, yt, ym, gt, wglu, bglu, wbs, wbm, wout, xg, wq, kv, wo)


def _conv_ffn_kernel(x_ref, fg_ref, wup_ref, cw_ref, cb_ref, wdn_ref, ng_ref, o_ref,
                     up_ref, act_ref, *, final_norm):
    tm = x_ref.shape[0]
    dff = wdn_ref.shape[0]
    ck = FFN_CHUNK
    n_chunks = dff // ck
    spc = ck // LANES
    base = 2 * FFN_PAD

    @pl.when(pl.program_id(1) == 0)
    def _():
        up_ref[:, 0:base, :] = jnp.zeros((up_ref.shape[0], base, LANES), F32)

    @pl.when(pl.program_id(1) > 0)
    def _():
        up_ref[:, 0:base, :] = up_ref[:, 2 * tm:2 * tm + base, :]

    x2 = x_ref[...]
    hb = _rmsnorm(x2, fg_ref[...]).astype(BF16)

    def up(c, half):
        c0 = half * dff + c * ck
        u = _dot(hb, wup_ref[:, c0:c0 + ck])
        for s in range(spc):
            up_ref[c * spc + s, pl.ds(base + half, tm, stride=2), :] = u[:, s * LANES:(s + 1) * LANES]

    def conv(c, half):
        c0 = half * dff + c * ck
        cols = []
        for s in range(spc):
            l0 = c0 + s * LANES
            acc = cb_ref[:, l0:l0 + LANES]
            for k in range(FFN_CONV):
                start = base + half - 2 * (FFN_CONV - 1 - k)
                acc = acc + (cw_ref[k:k + 1, l0:l0 + LANES]
                             * up_ref[c * spc + s, pl.ds(start, tm, stride=2), :])
            cols.append(acc)
        return jnp.concatenate(cols, axis=1)

    up(0, 0)
    up(0, 1)
    for c in range(n_chunks):
        if c + 1 < n_chunks:
            up(c + 1, 0)
            up(c + 1, 1)
        a = conv(c, 0)
        ab = a * conv(c, 1)
        th = jnp.tanh(a * (GELU_C1 + GELU_C3 * (a * a)))
        act_ref[:, c * ck:(c + 1) * ck] = (ab + ab * th).astype(BF16)
    y = x2 + _dot(act_ref[...], wdn_ref[...])
    o_ref[...] = _rmsnorm(y, ng_ref[...]) if final_norm else y


def _conv_ffn(x2, fg, wup, cw, cb, wdn, ng, n_batch, final_norm):
    n, d = x2.shape
    tm = TOKEN_TILE
    per_b = n // n_batch // tm
    row = pl.BlockSpec((tm, d), lambda b, i: (b * per_b + i, 0))
    return pl.pallas_call(
        functools.partial(_conv_ffn_kernel, final_norm=final_norm),
        grid=(n_batch, per_b),
        in_specs=[row, _full(fg.shape), _full(wup.shape), _full(cw.shape), _full(cb.shape),
                  _full(wdn.shape), _full(ng.shape)],
        out_specs=row,
        out_shape=jax.ShapeDtypeStruct((n, d), F32),
        scratch_shapes=[pltpu.VMEM((wdn.shape[0] // LANES, 2 * (FFN_PAD + tm), LANES), F32),
                        pltpu.VMEM((tm, wdn.shape[0]), BF16)],
        compiler_params=_params(("arbitrary", "arbitrary")),
        name="conv_ffn",
    )(x2, fg, wup, cw, cb, wdn, ng)


def kernel(x, mem, mix_norm_g, w_in, s5_lam_re, s5_lam_im, s5_b_re, s5_b_im, s5_c_re, s5_c_im,
           s5_d, s5_log_dt, s5_w_glu, s5_b_glu, m_conv_w, m_conv_b, m_b_i, m_b_f, m_norm_g,
           w_br_s5, w_br_m, b_gate, w_out, x_norm_g, mem_norm_g, x_wq, x_wkv, x_wo,
           f_norm_g, f_w_up, f_conv_w, f_conv_b, f_w_down, final_norm_g):
    B, L, D = x.shape
    N = B * L
    depth = w_in.shape[0]
    G = s5_lam_re.shape[1]
    s5w = G * S5_GROUP
    mw = (w_in.shape[2] - s5w - 2 * M_HEADS - 2 * D) // 4
    row = lambda a: a.reshape(1, -1)
    bf = lambda a: a.astype(BF16)

    xs = x.reshape(N, D)
    mem2d = mem.reshape(-1, D)
    for l in range(depth):
        wl = w_in[l]
        c0 = s5w
        c1 = c0 + 2 * mw
        c2 = c1 + mw
        c3 = c2 + mw
        c4 = c3 + 2 * M_HEADS
        wif = jnp.pad(wl[:, c3:c4], ((0, 0), (0, LANES - 2 * M_HEADS)))
        ut, qk, v, o, gif, gates = _in_proj(
            xs, row(mix_norm_g[l]), bf(wl[:, :c0]), bf(wl[:, c0:c1]), bf(wl[:, c1:c2]),
            bf(wl[:, c2:c3]), bf(wif), bf(wl[:, c4:]), row(b_gate[l]))

        bd, w_st, v_st, a16 = _s5_prep(
            s5_log_dt[l], s5_lam_re[l], s5_lam_im[l], s5_b_re[l], s5_b_im[l], s5_c_re[l],
            s5_c_im[l], s5_d[l])
        yt = _s5_scan(ut, bd, w_st, v_st, a16, B)

        bif = jnp.pad(jnp.concatenate([m_b_i[l], m_b_f[l]]), (0, LANES - 2 * M_HEADS)).reshape(1, LANES)
        y_m = _mlstm(qk, v, o, gif, m_conv_w[l], row(m_conv_b[l]), bif, row(m_norm_g[l]), B)

        kv = _mem_kv(mem2d, row(mem_norm_g[l]), bf(x_wkv[l])).reshape(B, -1, 2 * D)
        x2 = _merge_attn(xs, yt, y_m, gates, bf(s5_w_glu[l]), row(s5_b_glu[l]), bf(w_br_s5[l]),
                         bf(w_br_m[l]), bf(w_out[l]), row(x_norm_g[l]), bf(x_wq[l]), kv,
                         bf(x_wo[l]), B)
        dff = f_w_down.shape[1]
        half = jnp.concatenate([jnp.ones((dff,), F32), jnp.full((dff,), 0.5, F32)])
        xs = _conv_ffn(x2, row(f_norm_g[l]), bf(f_w_up[l]), f_conv_w[l] * half,
                       row(f_conv_b[l] * half), bf(f_w_down[l]), row(final_norm_g), B,
                       l == depth - 1)
    return xs.reshape(B, L, D)
```

```python
import functools
import math

import jax
import jax.numpy as jnp
from jax import lax
from jax.experimental import pallas as pl
from jax.experimental.pallas import tpu as pltpu

F32 = jnp.float32
BF16 = jnp.bfloat16

EPS = 1e-6
S5_GROUP = 16
S5_STATE = 64
S5_CHUNK = 16
M_HEADS = 4
M_CONV = 4
M_CHUNK = 128
M_STEP_CHUNKS = 8
M_PAD = 8
X_HEADS = 4
FFN_CONV = 3
NEG_BIG = -1e30

LANES = 128
S5_BLOCK = LANES // S5_GROUP
TOKEN_TILE = 512
PROJ_TILE = 1024
FFN_CHUNK = 256
FFN_PAD = 8
VMEM_LIMIT = 56 * 1024 * 1024


def _dot(a, b):
    return jnp.dot(a, b, preferred_element_type=F32)


def _dot_nt(a, b):
    return lax.dot_general(a, b, (((1,), (1,)), ((), ())), preferred_element_type=F32)


def _sigmoid(x):
    return 1.0 / (1.0 + jnp.exp(-x))


def _log_sigmoid(x):
    return jnp.minimum(x, 0.0) - jnp.log(1.0 + jnp.exp(-jnp.abs(x)))


GELU_C1 = math.sqrt(2.0 / math.pi)
GELU_C3 = 0.044715 * GELU_C1


def _gelu(x):
    return x * (0.5 * (1.0 + jnp.tanh(x * (GELU_C1 + GELU_C3 * (x * x)))))


def _rmsnorm(x, g):
    return x * lax.rsqrt(jnp.mean(x * x, axis=-1, keepdims=True) + EPS) * g


def _cmul(ar, ai, br, bi):
    return ar * br - ai * bi, ar * bi + ai * br


def _full(shape):
    n = len(shape)
    return pl.BlockSpec(shape, lambda *_: (0,) * n)


def _params(sem):
    return pltpu.CompilerParams(dimension_semantics=sem, vmem_limit_bytes=VMEM_LIMIT)


def _in_proj_kernel(x_ref, g_ref, wu_ref, wqk_ref, wv_ref, wo_ref, wif_ref, wg_ref, bg_ref,
                    ut_ref, qk_ref, v_ref, o_ref, if_ref, gt_ref, us_ref):
    hb = _rmsnorm(x_ref[...], g_ref[...]).astype(BF16)
    u = _dot(hb, wu_ref[...])
    n_slab = u.shape[1] // LANES
    for j in range(n_slab):
        us_ref[j] = u[:, j * LANES:(j + 1) * LANES]
    rows = u.shape[0] // S5_CHUNK
    for t in range(S5_CHUNK):
        for j in range(n_slab):
            ut_ref[t, :, j * LANES:(j + 1) * LANES] = (
                us_ref[j, pl.ds(t, rows, stride=S5_CHUNK), :].astype(BF16))
    qk_ref[...] = _dot(hb, wqk_ref[...]).astype(BF16)
    v_ref[...] = _dot(hb, wv_ref[...]).astype(BF16)
    o_ref[...] = _sigmoid(_dot(hb, wo_ref[...])).astype(BF16)
    if_ref[...] = _dot(hb, wif_ref[...])
    gt_ref[...] = _sigmoid(_dot(hb, wg_ref[...]) + bg_ref[...]).astype(BF16)


def _in_proj(x2d, g, wu, wqk, wv, wo, wif, wg, bg):
    n, d = x2d.shape
    tm = PROJ_TILE
    row = lambda c: pl.BlockSpec((tm, c), lambda i: (i, 0))
    outs = [(wqk.shape[1], BF16), (wv.shape[1], BF16), (wo.shape[1], BF16),
            (wif.shape[1], F32), (wg.shape[1], BF16)]
    s5w = wu.shape[1]
    T = S5_CHUNK
    return pl.pallas_call(
        _in_proj_kernel,
        grid=(n // tm,),
        in_specs=[row(d), _full(g.shape), _full(wu.shape), _full(wqk.shape), _full(wv.shape),
                  _full(wo.shape), _full(wif.shape), _full(wg.shape), _full(bg.shape)],
        out_specs=[pl.BlockSpec((T, tm // T, s5w), lambda i: (0, i, 0))] + [row(c) for c, _ in outs],
        out_shape=([jax.ShapeDtypeStruct((T, n // T, s5w), BF16)]
                   + [jax.ShapeDtypeStruct((n, c), dt) for c, dt in outs]),
        scratch_shapes=[pltpu.VMEM((s5w // LANES, tm, LANES), F32)],
        compiler_params=_params(("parallel",)),
        name="in_proj",
    )(x2d, g, wu, wqk, wv, wo, wif, wg, bg)


def _split3(x):
    hi = x.astype(BF16)
    r1 = x - hi.astype(F32)
    mid = r1.astype(BF16)
    lo = (r1 - mid.astype(F32)).astype(BF16)
    return hi, mid, lo


def _dot_x3(a, b):
    a_hi, a_lo, _ = _split3(a)
    b_hi, b_lo, _ = _split3(b)
    return _dot(a_hi, b_hi) + _dot(a_hi, b_lo) + _dot(a_lo, b_hi)


def _place(x, onehot):
    hi, mid, lo = _split3(x)
    return _dot(hi, onehot) + _dot(mid, onehot) + _dot(lo, onehot)


def _s5_prep_kernel(ldt_r_ref, lre_r_ref, lim_r_ref, ldt_c_ref, lre_c_ref, lim_c_ref,
                    btre_ref, btim_ref, ctre_ref, ctim_ref, d_ref,
                    bd_ref, w_ref, v_ref, a16_ref, pre_ref, pim_ref):
    P, H, T, M = S5_STATE, S5_GROUP, S5_CHUNK, S5_BLOCK
    MP, MH = M * P, M * H

    lr, li = lre_r_ref[0], lim_r_ref[0]
    dt_r = jnp.exp(ldt_r_ref[0])
    e_r = jnp.exp(lr * dt_r)
    ar_r, ai_r = e_r * jnp.cos(li * dt_r), e_r * jnp.sin(li * dt_r)
    dt_c = jnp.exp(ldt_c_ref[0])
    e_c = jnp.exp(lre_c_ref[0] * dt_c)
    ar_c, ai_c = e_c * jnp.cos(lim_c_ref[0] * dt_c), e_c * jnp.sin(lim_c_ref[0] * dt_c)

    t_r = lax.broadcasted_iota(jnp.int32, (P, MP), 0)
    t_l = lax.broadcasted_iota(jnp.int32, (P, MP), 1)
    tile_p = (jnp.bitwise_and(t_l, P - 1) == t_r).astype(BF16)
    b_r = lax.broadcasted_iota(jnp.int32, (MH, MP), 0)
    b_l = lax.broadcasted_iota(jnp.int32, (MH, MP), 1)
    sh_h, sh_p = H.bit_length() - 1, P.bit_length() - 1
    own_b = jnp.right_shift(b_r, sh_h) == jnp.right_shift(b_l, sh_p)
    bre = jnp.where(own_b, _place(btre_ref[0], tile_p), 0.0)
    bim = jnp.where(own_b, _place(btim_ref[0], tile_p), 0.0)
    inv = 1.0 / (lr * lr + li * li)
    cr = ((ar_r - 1.0) * lr + ai_r * li) * inv
    ci = (ai_r * lr - (ar_r - 1.0) * li) * inv
    bre, bim = _cmul(cr, ci, bre, bim)

    u_r = lax.broadcasted_iota(jnp.int32, (H, MH), 0)
    u_l = lax.broadcasted_iota(jnp.int32, (H, MH), 1)
    tile_h = (jnp.bitwise_and(u_l, H - 1) == u_r).astype(BF16)
    c_r = lax.broadcasted_iota(jnp.int32, (MP, MH), 0)
    c_l = lax.broadcasted_iota(jnp.int32, (MP, MH), 1)
    own_c = jnp.right_shift(c_r, sh_p) == jnp.right_shift(c_l, sh_h)
    cre = jnp.where(own_c, _place(ctre_ref[0], tile_h), 0.0)
    cim = jnp.where(own_c, _place(ctim_ref[0], tile_h), 0.0)

    qre, qim = cre, cim
    for tau in range(T + 1):
        pre_ref[:, tau * MH:(tau + 1) * MH] = qre
        pim_ref[:, tau * MH:(tau + 1) * MH] = qim
        if tau < T:
            qre, qim = _cmul(ar_c, ai_c, qre, qim)

    bd = _dot_x3(bre, pre_ref[:, 0:T * MH]) - _dot_x3(bim, pim_ref[:, 0:T * MH])
    d_r = lax.broadcasted_iota(jnp.int32, (MH, MH), 0)
    d_l = lax.broadcasted_iota(jnp.int32, (MH, MH), 1)
    bd_ref[0, :, 0:MH] = (bd[:, 0:MH] + jnp.where(d_r == d_l, d_ref[0], 0.0)).astype(BF16)
    bd_ref[0, :, MH:T * MH] = bd[:, MH:T * MH].astype(BF16)

    v_ref[0, 0:MP, :] = pre_ref[:, MH:(T + 1) * MH].astype(BF16)
    v_ref[0, MP:2 * MP, :] = (-pim_ref[:, MH:(T + 1) * MH]).astype(BF16)

    wre, wim = bre, bim
    for t in range(T - 1, -1, -1):
        w_ref[0, t * MH:(t + 1) * MH, 0:MP] = wre.astype(BF16)
        w_ref[0, t * MH:(t + 1) * MH, MP:2 * MP] = wim.astype(BF16)
        if t > 0:
            wre, wim = _cmul(ar_r, ai_r, wre, wim)

    sr, si = ar_r, ai_r
    for _ in range(4):
        sr, si = _cmul(sr, si, sr, si)
    row8 = lax.broadcasted_iota(jnp.int32, (8, MP), 0)
    a16_ref[0] = jnp.where(row8 == 0, sr, si)


def _s5_prep(log_dt, lam_re, lam_im, b_re, b_im, c_re, c_im, d):
    G, P = lam_re.shape
    H, T, M = S5_GROUP, S5_CHUNK, S5_BLOCK
    J = G // M
    MP, MH = M * P, M * H
    ldt = jnp.repeat(log_dt, P)
    args = [
        ldt.reshape(J, 1, MP), lam_re.reshape(J, 1, MP), lam_im.reshape(J, 1, MP),
        ldt.reshape(J, MP, 1), lam_re.reshape(J, MP, 1), lam_im.reshape(J, MP, 1),
        b_re.transpose(0, 2, 1).reshape(J, MH, P), b_im.transpose(0, 2, 1).reshape(J, MH, P),
        c_re.transpose(0, 2, 1).reshape(J, MP, H), c_im.transpose(0, 2, 1).reshape(J, MP, H),
        d.reshape(J, 1, MH),
    ]
    blk = lambda s: pl.BlockSpec((1,) + tuple(s[1:]), lambda j: (j, 0, 0))
    outs = [((J, MH, T * MH), BF16), ((J, T * MH, 2 * MP), BF16), ((J, 2 * MP, T * MH), BF16),
            ((J, 8, MP), F32)]
    return pl.pallas_call(
        _s5_prep_kernel,
        grid=(J,),
        in_specs=[blk(a.shape) for a in args],
        out_specs=[blk(s) for s, _ in outs],
        out_shape=[jax.ShapeDtypeStruct(s, dt) for s, dt in outs],
        scratch_shapes=[pltpu.VMEM((MP, (T + 1) * MH), F32)] * 2,
        compiler_params=_params(("parallel",)),
        name="s5_prep",
    )(*args)


def _s5_scan_kernel(ut_ref, bd_ref, w_ref, v_ref, a16_ref, yt_ref, big_ref, xs_ref, sp_ref):
    T = S5_CHUNK
    R = ut_ref.shape[1]
    MH = ut_ref.shape[2]
    MP = a16_ref.shape[2]

    @pl.when(pl.program_id(1) == 0)
    def _():
        for s in range(T):
            for t in range(T):
                blk = (bd_ref[0, :, (t - s) * MH:(t - s + 1) * MH] if t >= s
                       else jnp.zeros((MH, MH), BF16))
                big_ref[s * MH:(s + 1) * MH, t * MH:(t + 1) * MH] = blk

    x = jnp.concatenate([ut_ref[t] for t in range(T)], axis=1)
    xs_ref[...] = _dot(x, w_ref[0])
    a16 = a16_ref[0]
    a1 = (jnp.broadcast_to(a16[0:1], (8, MP)), jnp.broadcast_to(a16[1:2], (8, MP)))
    a2 = _cmul(*a1, *a1)
    a4 = _cmul(*a2, *a2)
    rid = lax.broadcasted_iota(jnp.int32, (8, MP), 0)
    pw = a1
    for r in range(1, 8):
        nxt = _cmul(*pw, *a1)
        pw = (jnp.where(rid >= r, nxt[0], pw[0]), jnp.where(rid >= r, nxt[1], pw[1]))

    def shift_rows(v, n):
        return jnp.where(rid >= n, pltpu.roll(v, n, 0), 0.0)

    def body(i, carry):
        sr, si = carry
        r0 = pl.multiple_of(i * 8, 8)
        yr = xs_ref[pl.ds(r0, 8), 0:MP]
        yi = xs_ref[pl.ds(r0, 8), MP:2 * MP]
        for n, am in ((1, a1), (2, a2), (4, a4)):
            tr, ti = _cmul(*am, shift_rows(yr, n), shift_rows(yi, n))
            yr, yi = yr + tr, yi + ti
        cr, ci = _cmul(*pw, jnp.broadcast_to(sr, (8, MP)), jnp.broadcast_to(si, (8, MP)))
        er, ei = yr + cr, yi + ci
        sp_ref[pl.ds(r0, 8), 0:MP] = jnp.where(rid == 0, sr, pltpu.roll(er, 1, 0))
        sp_ref[pl.ds(r0, 8), MP:2 * MP] = jnp.where(rid == 0, si, pltpu.roll(ei, 1, 0))
        return er[7:8], ei[7:8]

    zero = jnp.zeros((1, MP), F32)
    lax.fori_loop(0, R // 8, body, (zero, zero), unroll=2)

    sp = sp_ref[...].astype(BF16)
    for p in range(T // 2):
        k = 2 * (p + 1) * MH
        cols = slice(2 * p * MH, 2 * (p + 1) * MH)
        acc = _dot(x[:, 0:k], big_ref[0:k, cols]) + _dot(sp, v_ref[0, :, cols])
        yt_ref[2 * p] = acc[:, 0:MH].astype(BF16)
        yt_ref[2 * p + 1] = acc[:, MH:2 * MH].astype(BF16)


def _s5_scan(ut, bd, w, v, a16, n_batch):
    T, nrow, s5w = ut.shape
    R = nrow // n_batch
    MH = LANES
    MP = a16.shape[2]
    per_j = lambda s: pl.BlockSpec((1,) + tuple(s[1:]), lambda j, b: (j, 0, 0))
    io = pl.BlockSpec((T, R, MH), lambda j, b: (0, b, j))
    return pl.pallas_call(
        _s5_scan_kernel,
        grid=(s5w // MH, n_batch),
        in_specs=[io, per_j(bd.shape), per_j(w.shape), per_j(v.shape), per_j(a16.shape)],
        out_specs=io,
        out_shape=jax.ShapeDtypeStruct(ut.shape, BF16),
        scratch_shapes=[pltpu.VMEM((T * MH, T * MH), BF16),
                        pltpu.VMEM((R, 2 * MP), F32),
                        pltpu.VMEM((R, 2 * MP), F32)],
        compiler_params=_params(("arbitrary", "arbitrary")),
        name="s5_scan",
    )(ut, bd, w, v, a16)


def _mlstm_kernel(qk_ref, v_ref, o_ref, if_ref, cw_ref, cb_ref, bif_ref, ng_ref, y_ref,
                  cv_ref, cn_ref, m_ref):
    Lc = M_CHUNK
    S = qk_ref.shape[0]
    W = v_ref.shape[1]
    Dh = W // M_HEADS
    base = 2 * M_PAD

    @pl.when(pl.program_id(1) == 0)
    def _():
        cv_ref[:, 0:base, :] = jnp.zeros((cv_ref.shape[0], base, LANES), F32)
        cn_ref[...] = jnp.zeros(cn_ref.shape, F32)
        m_ref[...] = jnp.zeros(m_ref.shape, F32)

    @pl.when(pl.program_id(1) > 0)
    def _():
        cv_ref[:, 0:base, :] = cv_ref[:, 2 * S:2 * S + base, :]

    x = qk_ref[...].astype(F32)
    n_slab = x.shape[1] // LANES
    for s in range(n_slab):
        cv_ref[s // 2, pl.ds(base + s % 2, S, stride=2), :] = x[:, s * LANES:(s + 1) * LANES]
    cols = []
    for s in range(n_slab):
        l0 = s * LANES
        acc = cb_ref[:, l0:l0 + LANES]
        for k in range(M_CONV):
            start = base + s % 2 - 2 * (M_CONV - 1 - k)
            acc = acc + cw_ref[k:k + 1, l0:l0 + LANES] * cv_ref[s // 2, pl.ds(start, S, stride=2), :]
        cols.append(acc * _sigmoid(acc))
    qk = jnp.concatenate(cols, axis=1)

    n_ci = S // Lc
    inst = [(ci, h) for ci in range(n_ci) for h in range(M_HEADS)]
    rows_of = lambda b: slice(b * Lc, (b + 1) * Lc)
    stack = lambda parts: jnp.concatenate(parts, axis=0)
    rep = lambda parts: stack([jnp.broadcast_to(p, (Lc, Dh)) for p in parts])

    lane = lax.broadcasted_iota(jnp.int32, (Lc, Lc), 1)
    row = lax.broadcasted_iota(jnp.int32, (Lc, Lc), 0)
    causal1 = row >= lane
    tril = causal1.astype(BF16)
    ones_blk = jnp.ones((Lc, Dh), BF16)

    gm, bcu, gm_t, bcu_t = [], [], [], []
    for ci in range(n_ci):
        pre = if_ref[ci * Lc:(ci + 1) * Lc, :] + bif_ref[...]
        g = jnp.where(lane < M_HEADS, pre, jnp.where(lane < 2 * M_HEADS, _log_sigmoid(pre), 0.0))
        c = _place_left(tril, g)
        gm.append(g)
        bcu.append(c)
        gm_t.append(g.T)
        bcu_t.append(c.T)

    hs = lambda h: slice(h * Dh, (h + 1) * Dh)
    cs = lambda ci: slice(ci * Lc, (ci + 1) * Lc)
    q = stack([qk[cs(ci), hs(h)] for ci, h in inst])
    k = stack([qk[cs(ci), W + h * Dh:W + (h + 1) * Dh] for ci, h in inst]) * (Dh ** -0.5)
    v1 = [jnp.concatenate([v_ref[cs(ci), hs(h)], ones_blk], axis=1) for ci, h in inst]
    ig = rep([gm[ci][:, h:h + 1] for ci, h in inst])
    bc = rep([bcu[ci][:, M_HEADS + h:M_HEADS + h + 1] for ci, h in inst])
    g_c = [jnp.broadcast_to(bcu[ci][Lc - 1:Lc, M_HEADS + h:M_HEADS + h + 1], (1, Dh))
           for ci, h in inst]
    r_mat = stack([jnp.where(causal1, gm_t[ci][h:h + 1, :] - bcu_t[ci][M_HEADS + h:M_HEADS + h + 1, :],
                             NEG_BIG) for ci, h in inst])

    a_end = rep(g_c) - bc + ig
    m_loc = [jnp.max(a_end[rows_of(b)], axis=0, keepdims=True) for b in range(len(inst))]
    m_run = [m_ref[h][0:1, :] for h in range(M_HEADS)]
    m_prev, m_new = [], []
    for b, (ci, h) in enumerate(inst):
        m_prev.append(m_run[h])
        m_run[h] = jnp.maximum(g_c[b] + m_run[h], m_loc[b])
        m_new.append(m_run[h])
    w_rep = jnp.exp(a_end - rep(m_new))

    mp = rep(m_prev)
    mx = jnp.maximum(mp, jnp.broadcast_to(jnp.max(r_mat, axis=1, keepdims=True), mp.shape))
    m_t = bc + mx
    qb, kb = q.astype(BF16), k.astype(BF16)
    s_mat = stack([_dot_nt(qb[rows_of(b)], kb[rows_of(b)]) for b in range(len(inst))])
    s_mat = s_mat * jnp.exp(r_mat - mx)
    inter_w = jnp.exp(mp - mx)
    lhs = jnp.concatenate([s_mat.astype(BF16), (inter_w * q).astype(BF16)], axis=1)

    kw = k * w_rep
    upd = [_dot(kw[rows_of(b)].T.astype(BF16), v1[b]) for b in range(len(inst))]
    cn_run = [cn_ref[h] for h in range(M_HEADS)]
    nd = []
    for b, (ci, h) in enumerate(inst):
        rhs = jnp.concatenate([v1[b], cn_run[h].astype(BF16)], axis=0)
        nd.append(_dot(lhs[rows_of(b)], rhs))
        decay = jnp.exp(g_c[b] + m_prev[b] - m_new[b])
        cn_run[h] = jnp.concatenate([decay, decay], axis=1) * cn_run[h] + upd[b]
    nd = stack(nd)
    num = nd[:, :Dh]
    den = nd[:, Dh:]
    r = 1.0 / jnp.maximum(jnp.abs(den), jnp.exp(-m_t))
    z = stack([o_ref[cs(ci), hs(h)] for ci, h in inst]).astype(F32) * num
    zz_hi, zz_lo, _ = _split3(z * z)
    mz = (_dot(zz_hi, ones_blk) + _dot(zz_lo, ones_blk)) * (1.0 / Dh)
    hh = z * (r * lax.rsqrt(r * r * mz + EPS))
    for b, (ci, h) in enumerate(inst):
        y_ref[cs(ci), hs(h)] = (hh[rows_of(b)] * ng_ref[:, hs(h)]).astype(BF16)

    for h in range(M_HEADS):
        cn_ref[h] = cn_run[h]
        m_ref[h] = jnp.broadcast_to(m_run[h], m_ref.shape[1:])


def _place_left(onehot, x):
    hi, mid, lo = _split3(x)
    return _dot(onehot, hi) + _dot(onehot, mid) + _dot(onehot, lo)


def _mlstm(qk, v, o, gif, conv_w, conv_b, bif, norm_g, n_batch):
    n, w = v.shape
    rows = M_STEP_CHUNKS * M_CHUNK
    nc = n // n_batch // rows
    dh = w // M_HEADS
    row = lambda c: pl.BlockSpec((rows, c), lambda b, i: (b * nc + i, 0))
    return pl.pallas_call(
        _mlstm_kernel,
        grid=(n_batch, nc),
        in_specs=[row(2 * w), row(w), row(w), row(gif.shape[1]), _full(conv_w.shape),
                  _full(conv_b.shape), _full(bif.shape), _full(norm_g.shape)],
        out_specs=row(w),
        out_shape=jax.ShapeDtypeStruct((n, w), BF16),
        scratch_shapes=[pltpu.VMEM((2 * w // LANES // 2, 2 * (M_PAD + rows), LANES), F32),
                        pltpu.VMEM((M_HEADS, dh, 2 * dh), F32),
                        pltpu.VMEM((M_HEADS, 8, LANES), F32)],
        compiler_params=_params(("arbitrary", "arbitrary")),
        name="mlstm",
    )(qk, v, o, gif, conv_w, conv_b, bif, norm_g)


def _mem_kv_kernel(mem_ref, g_ref, wkv_ref, kv_ref):
    kv_ref[...] = _dot(_rmsnorm(mem_ref[...], g_ref[...]).astype(BF16), wkv_ref[...]).astype(BF16)


def _mem_kv(mem2d, g, wkv):
    n, d = mem2d.shape
    return pl.pallas_call(
        _mem_kv_kernel,
        grid=(1,),
        in_specs=[_full(mem2d.shape), _full(g.shape), _full(wkv.shape)],
        out_specs=_full((n, wkv.shape[1])),
        out_shape=jax.ShapeDtypeStruct((n, wkv.shape[1]), BF16),
        compiler_params=_params(("arbitrary",)),
        name="mem_kv",
    )(mem2d, g, wkv)


def _merge_attn_kernel(x_ref, yt_ref, ym_ref, gt_ref, wglu_ref, bglu_ref, wbs_ref, wbm_ref,
                       wout_ref, xg_ref, wq_ref, kv_ref, wo_ref, x2_ref, ys_ref):
    d = x_ref.shape[1]
    rows = yt_ref.shape[1]
    n_slab = yt_ref.shape[2] // LANES
    p_m = _dot(ym_ref[...], wbm_ref[...])
    for t in range(S5_CHUNK):
        for j in range(n_slab):
            ys_ref[j, pl.ds(t, rows, stride=S5_CHUNK), :] = (
                yt_ref[t, :, j * LANES:(j + 1) * LANES].astype(F32))
    ys = _gelu(jnp.concatenate([ys_ref[j] for j in range(n_slab)], axis=1))
    ys = ys * _sigmoid(_dot(ys.astype(BF16), wglu_ref[...]) + bglu_ref[...])
    p_s = _dot(ys.astype(BF16), wbs_ref[...])
    merged = gt_ref[:, :d].astype(F32) * p_s + gt_ref[:, d:].astype(F32) * p_m
    x1 = x_ref[...] + _dot(merged.astype(BF16), wout_ref[...])

    q = _dot(_rmsnorm(x1, xg_ref[...]).astype(BF16), wq_ref[...]).astype(BF16)
    dh = d // X_HEADS
    outs = []
    for h in range(X_HEADS):
        kh = kv_ref[0, :, h * dh:(h + 1) * dh]
        vh = kv_ref[0, :, d + h * dh:d + (h + 1) * dh]
        s = _dot_nt(q[:, h * dh:(h + 1) * dh], kh) * (dh ** -0.5)
        s = s - jnp.max(s, axis=-1, keepdims=True)
        p = jnp.exp(s)
        p = p / jnp.sum(p, axis=-1, keepdims=True)
        outs.append(_dot(p.astype(BF16), vh).astype(BF16))
    o = jnp.concatenate(outs, axis=1)
    x2_ref[...] = x1 + _dot(o, wo_ref[...])


def _merge_attn(x2d, yt, ym, gt, wglu, bglu, wbs, wbm, wout, xg, wq, kv, wo, n_batch):
    n, d = x2d.shape
    tm = PROJ_TILE
    per_b = n // n_batch // tm
    row = lambda c: pl.BlockSpec((tm, c), lambda b, i: (b * per_b + i, 0))
    n_mem = kv.shape[1]
    T, _, s5w = yt.shape
    return pl.pallas_call(
        _merge_attn_kernel,
        grid=(n_batch, per_b),
        in_specs=[row(d), pl.BlockSpec((T, tm // T, s5w), lambda b, i: (0, b * per_b + i, 0)),
                  row(ym.shape[1]), row(gt.shape[1]),
                  _full(wglu.shape), _full(bglu.shape), _full(wbs.shape), _full(wbm.shape),
                  _full(wout.shape), _full(xg.shape), _full(wq.shape),
                  pl.BlockSpec((1, n_mem, kv.shape[2]), lambda b, i: (b, 0, 0)),
                  _full(wo.shape)],
        out_specs=row(d),
        out_shape=jax.ShapeDtypeStruct((n, d), F32),
        scratch_shapes=[pltpu.VMEM((s5w // LANES, tm, LANES), F32)],
        compiler_params=_params(("parallel", "parallel")),
        name="merge_attn",
    )(x2d, yt, ym, gt, wglu, bglu, wbs, wbm, wout, xg, wq, kv, wo)


def _conv_ffn_kernel(x_ref, fg_ref, wup_ref, cw_ref, cb_ref, wdn_ref, ng_ref, o_ref,
                     up_ref, act_ref, *, final_norm):
    tm = x_ref.shape[0]
    dff = wdn_ref.shape[0]
    ck = FFN_CHUNK
    n_chunks = dff // ck
    spc = ck // LANES
    base = 2 * FFN_PAD

    @pl.when(pl.program_id(1) == 0)
    def _():
        up_ref[:, 0:base, :] = jnp.zeros((up_ref.shape[0], base, LANES), F32)

    @pl.when(pl.program_id(1) > 0)
    def _():
        up_ref[:, 0:base, :] = up_ref[:, 2 * tm:2 * tm + base, :]

    x2 = x_ref[...]
    hb = _rmsnorm(x2, fg_ref[...]).astype(BF16)

    def up(c, half):
        c0 = half * dff + c * ck
        u = _dot(hb, wup_ref[:, c0:c0 + ck])
        for s in range(spc):
            up_ref[c * spc + s, pl.ds(base + half, tm, stride=2), :] = u[:, s * LANES:(s + 1) * LANES]

    def conv(c, half):
        c0 = half * dff + c * ck
        cols = []
        for s in range(spc):
            l0 = c0 + s * LANES
            acc = cb_ref[:, l0:l0 + LANES]
            for k in range(FFN_CONV):
                start = base + half - 2 * (FFN_CONV - 1 - k)
                acc = acc + (cw_ref[k:k + 1, l0:l0 + LANES]
                             * up_ref[c * spc + s, pl.ds(start, tm, stride=2), :])
            cols.append(acc)
        return jnp.concatenate(cols, axis=1)

    up(0, 0)
    up(0, 1)
    for c in range(n_chunks):
        if c + 1 < n_chunks:
            up(c + 1, 0)
            up(c + 1, 1)
        a = conv(c, 0)
        ab = a * conv(c, 1)
        th = jnp.tanh(a * (GELU_C1 + GELU_C3 * (a * a)))
        act_ref[:, c * ck:(c + 1) * ck] = (ab + ab * th).astype(BF16)
    y = x2 + _dot(act_ref[...], wdn_ref[...])
    o_ref[...] = _rmsnorm(y, ng_ref[...]) if final_norm else y


def _conv_ffn(x2, fg, wup, cw, cb, wdn, ng, n_batch, final_norm):
    n, d = x2.shape
    tm = TOKEN_TILE
    per_b = n // n_batch // tm
    row = pl.BlockSpec((tm, d), lambda b, i: (b * per_b + i, 0))
    return pl.pallas_call(
        functools.partial(_conv_ffn_kernel, final_norm=final_norm),
        grid=(n_batch, per_b),
        in_specs=[row, _full(fg.shape), _full(wup.shape), _full(cw.shape), _full(cb.shape),
                  _full(wdn.shape), _full(ng.shape)],
        out_specs=row,
        out_shape=jax.ShapeDtypeStruct((n, d), F32),
        scratch_shapes=[pltpu.VMEM((wdn.shape[0] // LANES, 2 * (FFN_PAD + tm), LANES), F32),
                        pltpu.VMEM((tm, wdn.shape[0]), BF16)],
        compiler_params=_params(("arbitrary", "arbitrary")),
        name="conv_ffn",
    )(x2, fg, wup, cw, cb, wdn, ng)


def kernel(x, mem, mix_norm_g, w_in, s5_lam_re, s5_lam_im, s5_b_re, s5_b_im, s5_c_re, s5_c_im,
           s5_d, s5_log_dt, s5_w_glu, s5_b_glu, m_conv_w, m_conv_b, m_b_i, m_b_f, m_norm_g,
           w_br_s5, w_br_m, b_gate, w_out, x_norm_g, mem_norm_g, x_wq, x_wkv, x_wo,
           f_norm_g, f_w_up, f_conv_w, f_conv_b, f_w_down, final_norm_g):
    B, L, D = x.shape
    N = B * L
    depth = w_in.shape[0]
    G = s5_lam_re.shape[1]
    s5w = G * S5_GROUP
    mw = (w_in.shape[2] - s5w - 2 * M_HEADS - 2 * D) // 4
    row = lambda a: a.reshape(1, -1)
    bf = lambda a: a.astype(BF16)

    xs = x.reshape(N, D)
    mem2d = mem.reshape(-1, D)
    for l in range(depth):
        wl = w_in[l]
        c0 = s5w
        c1 = c0 + 2 * mw
        c2 = c1 + mw
        c3 = c2 + mw
        c4 = c3 + 2 * M_HEADS
        wif = jnp.pad(wl[:, c3:c4], ((0, 0), (0, LANES - 2 * M_HEADS)))
        ut, qk, v, o, gif, gates = _in_proj(
            xs, row(mix_norm_g[l]), bf(wl[:, :c0]), bf(wl[:, c0:c1]), bf(wl[:, c1:c2]),
            bf(wl[:, c2:c3]), bf(wif), bf(wl[:, c4:]), row(b_gate[l]))

        bd, w_st, v_st, a16 = _s5_prep(
            s5_log_dt[l], s5_lam_re[l], s5_lam_im[l], s5_b_re[l], s5_b_im[l], s5_c_re[l],
            s5_c_im[l], s5_d[l])
        yt = _s5_scan(ut, bd, w_st, v_st, a16, B)

        bif = jnp.pad(jnp.concatenate([m_b_i[l], m_b_f[l]]), (0, LANES - 2 * M_HEADS)).reshape(1, LANES)
        y_m = _mlstm(qk, v, o, gif, m_conv_w[l], row(m_conv_b[l]), bif, row(m_norm_g[l]), B)

        kv = _mem_kv(mem2d, row(mem_norm_g[l]), bf(x_wkv[l])).reshape(B, -1, 2 * D)
        x2 = _merge_attn(xs, yt, y_m, gates, bf(s5_w_glu[l]), row(s5_b_glu[l]), bf(w_br_s5[l]),
                         bf(w_br_m[l]), bf(w_out[l]), row(x_norm_g[l]), bf(x_wq[l]), kv,
                         bf(x_wo[l]), B)
        dff = f_w_down.shape[1]
        half = jnp.concatenate([jnp.ones((dff,), F32), jnp.full((dff,), 0.5, F32)])
        xs = _conv_ffn(x2, row(f_norm_g[l]), bf(f_w_up[l]), f_conv_w[l] * half,
                       row(f_conv_b[l] * half), bf(f_w_down[l]), row(final_norm_g), B,
                       l == depth - 1)
    return xs.reshape(B, L, D)
```

```python
import functools
import math

import jax
import jax.numpy as jnp
from jax import lax
from jax.experimental import pallas as pl
from jax.experimental.pallas import tpu as pltpu

F32 = jnp.float32
BF16 = jnp.bfloat16

EPS = 1e-6
S5_GROUP = 16
S5_STATE = 64
S5_CHUNK = 16
M_HEADS = 4
M_CONV = 4
M_CHUNK = 128
M_STEP_CHUNKS = 8
M_PAD = 8
X_HEADS = 4
FFN_CONV = 3
NEG_BIG = -1e30

LANES = 128
S5_BLOCK = LANES // S5_GROUP
TOKEN_TILE = 512
PROJ_TILE = 1024
FFN_CHUNK = 256
FFN_PAD = 8
VMEM_LIMIT = 56 * 1024 * 1024


def _dot(a, b):
    return jnp.dot(a, b, preferred_element_type=F32)


def _dot_nt(a, b):
    return lax.dot_general(a, b, (((1,), (1,)), ((), ())), preferred_element_type=F32)


def _sigmoid(x):
    return 1.0 / (1.0 + jnp.exp(-x))


def _log_sigmoid(x):
    return jnp.minimum(x, 0.0) - jnp.log(1.0 + jnp.exp(-jnp.abs(x)))


GELU_C1 = math.sqrt(2.0 / math.pi)
GELU_C3 = 0.044715 * GELU_C1


def _gelu(x):
    return x * (0.5 * (1.0 + jnp.tanh(x * (GELU_C1 + GELU_C3 * (x * x)))))


def _rmsnorm(x, g):
    return x * lax.rsqrt(jnp.mean(x * x, axis=-1, keepdims=True) + EPS) * g


def _cmul(ar, ai, br, bi):
    return ar * br - ai * bi, ar * bi + ai * br


def _full(shape):
    n = len(shape)
    return pl.BlockSpec(shape, lambda *_: (0,) * n)


def _params(sem):
    return pltpu.CompilerParams(dimension_semantics=sem, vmem_limit_bytes=VMEM_LIMIT)


def _in_proj_kernel(x_ref, g_ref, wu_ref, wqk_ref, wv_ref, wo_ref, wif_ref, wg_ref, bg_ref,
                    ut_ref, qk_ref, v_ref, o_ref, if_ref, gt_ref, us_ref):
    hb = _rmsnorm(x_ref[...], g_ref[...]).astype(BF16)
    u = _dot(hb, wu_ref[...])
    n_slab = u.shape[1] // LANES
    for j in range(n_slab):
        us_ref[j] = u[:, j * LANES:(j + 1) * LANES]
    rows = u.shape[0] // S5_CHUNK
    for t in range(S5_CHUNK):
        for j in range(n_slab):
            ut_ref[t, :, j * LANES:(j + 1) * LANES] = (
                us_ref[j, pl.ds(t, rows, stride=S5_CHUNK), :].astype(BF16))
    qk_ref[...] = _dot(hb, wqk_ref[...]).astype(BF16)
    v_ref[...] = _dot(hb, wv_ref[...]).astype(BF16)
    o_ref[...] = _sigmoid(_dot(hb, wo_ref[...])).astype(BF16)
    if_ref[...] = _dot(hb, wif_ref[...])
    gt_ref[...] = _sigmoid(_dot(hb, wg_ref[...]) + bg_ref[...]).astype(BF16)


def _in_proj(x2d, g, wu, wqk, wv, wo, wif, wg, bg):
    n, d = x2d.shape
    tm = PROJ_TILE
    row = lambda c: pl.BlockSpec((tm, c), lambda i: (i, 0))
    outs = [(wqk.shape[1], BF16), (wv.shape[1], BF16), (wo.shape[1], BF16),
            (wif.shape[1], F32), (wg.shape[1], BF16)]
    s5w = wu.shape[1]
    T = S5_CHUNK
    return pl.pallas_call(
        _in_proj_kernel,
        grid=(n // tm,),
        in_specs=[row(d), _full(g.shape), _full(wu.shape), _full(wqk.shape), _full(wv.shape),
                  _full(wo.shape), _full(wif.shape), _full(wg.shape), _full(bg.shape)],
        out_specs=[pl.BlockSpec((T, tm // T, s5w), lambda i: (0, i, 0))] + [row(c) for c, _ in outs],
        out_shape=([jax.ShapeDtypeStruct((T, n // T, s5w), BF16)]
                   + [jax.ShapeDtypeStruct((n, c), dt) for c, dt in outs]),
        scratch_shapes=[pltpu.VMEM((s5w // LANES, tm, LANES), F32)],
        compiler_params=_params(("parallel",)),
        name="in_proj",
    )(x2d, g, wu, wqk, wv, wo, wif, wg, bg)


def _split3(x):
    hi = x.astype(BF16)
    r1 = x - hi.astype(F32)
    mid = r1.astype(BF16)
    lo = (r1 - mid.astype(F32)).astype(BF16)
    return hi, mid, lo


def _dot_x3(a, b):
    a_hi, a_lo, _ = _split3(a)
    b_hi, b_lo, _ = _split3(b)
    return _dot(a_hi, b_hi) + _dot(a_hi, b_lo) + _dot(a_lo, b_hi)


def _place(x, onehot):
    hi, mid, lo = _split3(x)
    return _dot(hi, onehot) + _dot(mid, onehot) + _dot(lo, onehot)


def _s5_prep_kernel(ldt_r_ref, lre_r_ref, lim_r_ref, ldt_c_ref, lre_c_ref, lim_c_ref,
                    btre_ref, btim_ref, ctre_ref, ctim_ref, d_ref,
                    bd_ref, w_ref, v_ref, a16_ref, pre_ref, pim_ref):
    P, H, T, M = S5_STATE, S5_GROUP, S5_CHUNK, S5_BLOCK
    MP, MH = M * P, M * H

    lr, li = lre_r_ref[0], lim_r_ref[0]
    dt_r = jnp.exp(ldt_r_ref[0])
    e_r = jnp.exp(lr * dt_r)
    ar_r, ai_r = e_r * jnp.cos(li * dt_r), e_r * jnp.sin(li * dt_r)
    dt_c = jnp.exp(ldt_c_ref[0])
    e_c = jnp.exp(lre_c_ref[0] * dt_c)
    ar_c, ai_c = e_c * jnp.cos(lim_c_ref[0] * dt_c), e_c * jnp.sin(lim_c_ref[0] * dt_c)

    t_r = lax.broadcasted_iota(jnp.int32, (P, MP), 0)
    t_l = lax.broadcasted_iota(jnp.int32, (P, MP), 1)
    tile_p = (jnp.bitwise_and(t_l, P - 1) == t_r).astype(BF16)
    b_r = lax.broadcasted_iota(jnp.int32, (MH, MP), 0)
    b_l = lax.broadcasted_iota(jnp.int32, (MH, MP), 1)
    sh_h, sh_p = H.bit_length() - 1, P.bit_length() - 1
    own_b = jnp.right_shift(b_r, sh_h) == jnp.right_shift(b_l, sh_p)
    bre = jnp.where(own_b, _place(btre_ref[0], tile_p), 0.0)
    bim = jnp.where(own_b, _place(btim_ref[0], tile_p), 0.0)
    inv = 1.0 / (lr * lr + li * li)
    cr = ((ar_r - 1.0) * lr + ai_r * li) * inv
    ci = (ai_r * lr - (ar_r - 1.0) * li) * inv
    bre, bim = _cmul(cr, ci, bre, bim)

    u_r = lax.broadcasted_iota(jnp.int32, (H, MH), 0)
    u_l = lax.broadcasted_iota(jnp.int32, (H, MH), 1)
    tile_h = (jnp.bitwise_and(u_l, H - 1) == u_r).astype(BF16)
    c_r = lax.broadcasted_iota(jnp.int32, (MP, MH), 0)
    c_l = lax.broadcasted_iota(jnp.int32, (MP, MH), 1)
    own_c = jnp.right_shift(c_r, sh_p) == jnp.right_shift(c_l, sh_h)
    cre = jnp.where(own_c, _place(ctre_ref[0], tile_h), 0.0)
    cim = jnp.where(own_c, _place(ctim_ref[0], tile_h), 0.0)

    qre, qim = cre, cim
    for tau in range(T + 1):
        pre_ref[:, tau * MH:(tau + 1) * MH] = qre
        pim_ref[:, tau * MH:(tau + 1) * MH] = qim
        if tau < T:
            qre, qim = _cmul(ar_c, ai_c, qre, qim)

    bd = _dot_x3(bre, pre_ref[:, 0:T * MH]) - _dot_x3(bim, pim_ref[:, 0:T * MH])
    d_r = lax.broadcasted_iota(jnp.int32, (MH, MH), 0)
    d_l = lax.broadcasted_iota(jnp.int32, (MH, MH), 1)
    bd_ref[0, :, 0:MH] = (bd[:, 0:MH] + jnp.where(d_r == d_l, d_ref[0], 0.0)).astype(BF16)
    bd_ref[0, :, MH:T * MH] = bd[:, MH:T * MH].astype(BF16)

    v_ref[0, 0:MP, :] = pre_ref[:, MH:(T + 1) * MH].astype(BF16)
    v_ref[0, MP:2 * MP, :] = (-pim_ref[:, MH:(T + 1) * MH]).astype(BF16)

    wre, wim = bre, bim
    for t in range(T - 1, -1, -1):
        w_ref[0, t * MH:(t + 1) * MH, 0:MP] = wre.astype(BF16)
        w_ref[0, t * MH:(t + 1) * MH, MP:2 * MP] = wim.astype(BF16)
        if t > 0:
            wre, wim = _cmul(ar_r, ai_r, wre, wim)

    sr, si = ar_r, ai_r
    for _ in range(4):
        sr, si = _cmul(sr, si, sr, si)
    row8 = lax.broadcasted_iota(jnp.int32, (8, MP), 0)
    a16_ref[0] = jnp.where(row8 == 0, sr, si)


def _s5_prep(log_dt, lam_re, lam_im, b_re, b_im, c_re, c_im, d):
    G, P = lam_re.shape
    H, T, M = S5_GROUP, S5_CHUNK, S5_BLOCK
    J = G // M
    MP, MH = M * P, M * H
    ldt = jnp.repeat(log_dt, P)
    args = [
        ldt.reshape(J, 1, MP), lam_re.reshape(J, 1, MP), lam_im.reshape(J, 1, MP),
        ldt.reshape(J, MP, 1), lam_re.reshape(J, MP, 1), lam_im.reshape(J, MP, 1),
        b_re.transpose(0, 2, 1).reshape(J, MH, P), b_im.transpose(0, 2, 1).reshape(J, MH, P),
        c_re.transpose(0, 2, 1).reshape(J, MP, H), c_im.transpose(0, 2, 1).reshape(J, MP, H),
        d.reshape(J, 1, MH),
    ]
    blk = lambda s: pl.BlockSpec((1,) + tuple(s[1:]), lambda j: (j, 0, 0))
    outs = [((J, MH, T * MH), BF16), ((J, T * MH, 2 * MP), BF16), ((J, 2 * MP, T * MH), BF16),
            ((J, 8, MP), F32)]
    return pl.pallas_call(
        _s5_prep_kernel,
        grid=(J,),
        in_specs=[blk(a.shape) for a in args],
        out_specs=[blk(s) for s, _ in outs],
        out_shape=[jax.ShapeDtypeStruct(s, dt) for s, dt in outs],
        scratch_shapes=[pltpu.VMEM((MP, (T + 1) * MH), F32)] * 2,
        compiler_params=_params(("parallel",)),
        name="s5_prep",
    )(*args)


def _s5_scan_kernel(ut_ref, bd_ref, w_ref, v_ref, a16_ref, yt_ref, big_ref, xs_ref, sp_ref,
                    part_ref):
    T = S5_CHUNK
    R = ut_ref.shape[1]
    MH = ut_ref.shape[2]
    MP = a16_ref.shape[2]

    @pl.when(pl.program_id(1) == 0)
    def _():
        for s in range(T):
            for t in range(T):
                blk = (bd_ref[0, :, (t - s) * MH:(t - s + 1) * MH] if t >= s
                       else jnp.zeros((MH, MH), BF16))
                big_ref[s * MH:(s + 1) * MH, t * MH:(t + 1) * MH] = blk

    x = jnp.concatenate([ut_ref[t] for t in range(T)], axis=1)
    xs_ref[...] = _dot(x, w_ref[0])
    a16 = a16_ref[0]
    a1 = (jnp.broadcast_to(a16[0:1], (8, MP)), jnp.broadcast_to(a16[1:2], (8, MP)))
    a2 = _cmul(*a1, *a1)
    a4 = _cmul(*a2, *a2)
    rid = lax.broadcasted_iota(jnp.int32, (8, MP), 0)
    pw = a1
    for r in range(1, 8):
        nxt = _cmul(*pw, *a1)
        pw = (jnp.where(rid >= r, nxt[0], pw[0]), jnp.where(rid >= r, nxt[1], pw[1]))

    def shift_rows(v, n):
        return jnp.where(rid >= n, pltpu.roll(v, n, 0), 0.0)

    def body(i, carry):
        sr, si = carry
        r0 = i * 8
        yr = xs_ref[r0:r0 + 8, 0:MP]
        yi = xs_ref[r0:r0 + 8, MP:2 * MP]
        for n, am in ((1, a1), (2, a2), (4, a4)):
            tr, ti = _cmul(*am, shift_rows(yr, n), shift_rows(yi, n))
            yr, yi = yr + tr, yi + ti
        cr, ci = _cmul(*pw, jnp.broadcast_to(sr, (8, MP)), jnp.broadcast_to(si, (8, MP)))
        er, ei = yr + cr, yi + ci
        sp_ref[r0:r0 + 8, 0:MP] = jnp.where(rid == 0, sr, pltpu.roll(er, 1, 0))
        sp_ref[r0:r0 + 8, MP:2 * MP] = jnp.where(rid == 0, si, pltpu.roll(ei, 1, 0))
        return er[7:8], ei[7:8]

    for p in range(T // 2):
        k = 2 * (p + 1) * MH
        cols = slice(2 * p * MH, 2 * (p + 1) * MH)
        part_ref[:, cols] = _dot(x[:, 0:k], big_ref[0:k, cols])

    carry = (jnp.zeros((1, MP), F32), jnp.zeros((1, MP), F32))
    for i in range(R // 8):
        carry = body(i, carry)

    sp = sp_ref[...].astype(BF16)
    for p in range(T // 2):
        cols = slice(2 * p * MH, 2 * (p + 1) * MH)
        acc = part_ref[:, cols] + _dot(sp, v_ref[0, :, cols])
        yt_ref[2 * p] = acc[:, 0:MH].astype(BF16)
        yt_ref[2 * p + 1] = acc[:, MH:2 * MH].astype(BF16)


def _s5_scan(ut, bd, w, v, a16, n_batch):
    T, nrow, s5w = ut.shape
    R = nrow // n_batch
    MH = LANES
    MP = a16.shape[2]
    per_j = lambda s: pl.BlockSpec((1,) + tuple(s[1:]), lambda j, b: (j, 0, 0))
    io = pl.BlockSpec((T, R, MH), lambda j, b: (0, b, j))
    return pl.pallas_call(
        _s5_scan_kernel,
        grid=(s5w // MH, n_batch),
        in_specs=[io, per_j(bd.shape), per_j(w.shape), per_j(v.shape), per_j(a16.shape)],
        out_specs=io,
        out_shape=jax.ShapeDtypeStruct(ut.shape, BF16),
        scratch_shapes=[pltpu.VMEM((T * MH, T * MH), BF16),
                        pltpu.VMEM((R, 2 * MP), F32),
                        pltpu.VMEM((R, 2 * MP), F32),
                        pltpu.VMEM((R, T * MH), F32)],
        compiler_params=_params(("arbitrary", "arbitrary")),
        name="s5_scan",
    )(ut, bd, w, v, a16)


def _mlstm_kernel(qk_ref, v_ref, o_ref, if_ref, cw_ref, cb_ref, bif_ref, ng_ref, y_ref,
                  cv_ref, cn_ref, m_ref):
    Lc = M_CHUNK
    S = qk_ref.shape[0]
    W = v_ref.shape[1]
    Dh = W // M_HEADS
    base = 2 * M_PAD

    @pl.when(pl.program_id(1) == 0)
    def _():
        cv_ref[:, 0:base, :] = jnp.zeros((cv_ref.shape[0], base, LANES), F32)
        cn_ref[...] = jnp.zeros(cn_ref.shape, F32)
        m_ref[...] = jnp.zeros(m_ref.shape, F32)

    @pl.when(pl.program_id(1) > 0)
    def _():
        cv_ref[:, 0:base, :] = cv_ref[:, 2 * S:2 * S + base, :]

    x = qk_ref[...].astype(F32)
    n_slab = x.shape[1] // LANES
    for s in range(n_slab):
        cv_ref[s // 2, pl.ds(base + s % 2, S, stride=2), :] = x[:, s * LANES:(s + 1) * LANES]
    cols = []
    for s in range(n_slab):
        l0 = s * LANES
        acc = cb_ref[:, l0:l0 + LANES]
        for k in range(M_CONV):
            start = base + s % 2 - 2 * (M_CONV - 1 - k)
            acc = acc + cw_ref[k:k + 1, l0:l0 + LANES] * cv_ref[s // 2, pl.ds(start, S, stride=2), :]
        cols.append(acc * _sigmoid(acc))
    qk = jnp.concatenate(cols, axis=1)

    n_ci = S // Lc
    inst = [(ci, h) for ci in range(n_ci) for h in range(M_HEADS)]
    rows_of = lambda b: slice(b * Lc, (b + 1) * Lc)
    stack = lambda parts: jnp.concatenate(parts, axis=0)
    rep = lambda parts: stack([jnp.broadcast_to(p, (Lc, Dh)) for p in parts])

    lane = lax.broadcasted_iota(jnp.int32, (Lc, Lc), 1)
    row = lax.broadcasted_iota(jnp.int32, (Lc, Lc), 0)
    causal1 = row >= lane
    tril = causal1.astype(BF16)
    ones_blk = jnp.ones((Lc, Dh), BF16)

    gm, bcu, gm_t, bcu_t = [], [], [], []
    for ci in range(n_ci):
        pre = if_ref[ci * Lc:(ci + 1) * Lc, :] + bif_ref[...]
        g = jnp.where(lane < M_HEADS, pre, jnp.where(lane < 2 * M_HEADS, _log_sigmoid(pre), 0.0))
        c = _place_left(tril, g)
        gm.append(g)
        bcu.append(c)
        gm_t.append(g.T)
        bcu_t.append(c.T)

    hs = lambda h: slice(h * Dh, (h + 1) * Dh)
    cs = lambda ci: slice(ci * Lc, (ci + 1) * Lc)
    q = stack([qk[cs(ci), hs(h)] for ci, h in inst])
    k = stack([qk[cs(ci), W + h * Dh:W + (h + 1) * Dh] for ci, h in inst]) * (Dh ** -0.5)
    v1 = [jnp.concatenate([v_ref[cs(ci), hs(h)], ones_blk], axis=1) for ci, h in inst]
    ig = rep([gm[ci][:, h:h + 1] for ci, h in inst])
    bc = rep([bcu[ci][:, M_HEADS + h:M_HEADS + h + 1] for ci, h in inst])
    g_c = [jnp.broadcast_to(bcu[ci][Lc - 1:Lc, M_HEADS + h:M_HEADS + h + 1], (1, Dh))
           for ci, h in inst]
    r_mat = stack([jnp.where(causal1, gm_t[ci][h:h + 1, :] - bcu_t[ci][M_HEADS + h:M_HEADS + h + 1, :],
                             NEG_BIG) for ci, h in inst])

    a_end = rep(g_c) - bc + ig
    m_loc = [jnp.max(a_end[rows_of(b)], axis=0, keepdims=True) for b in range(len(inst))]
    m_run = [m_ref[h][0:1, :] for h in range(M_HEADS)]
    m_prev, m_new = [], []
    for b, (ci, h) in enumerate(inst):
        m_prev.append(m_run[h])
        m_run[h] = jnp.maximum(g_c[b] + m_run[h], m_loc[b])
        m_new.append(m_run[h])
    w_rep = jnp.exp(a_end - rep(m_new))

    mp = rep(m_prev)
    mx = jnp.maximum(mp, jnp.broadcast_to(jnp.max(r_mat, axis=1, keepdims=True), mp.shape))
    m_t = bc + mx
    qb, kb = q.astype(BF16), k.astype(BF16)
    s_mat = stack([_dot_nt(qb[rows_of(b)], kb[rows_of(b)]) for b in range(len(inst))])
    s_mat = s_mat * jnp.exp(r_mat - mx)
    inter_w = jnp.exp(mp - mx)
    lhs = jnp.concatenate([s_mat.astype(BF16), (inter_w * q).astype(BF16)], axis=1)

    kw = k * w_rep
    upd = [_dot(kw[rows_of(b)].T.astype(BF16), v1[b]) for b in range(len(inst))]
    cn_run = [cn_ref[h] for h in range(M_HEADS)]
    nd = []
    for b, (ci, h) in enumerate(inst):
        rhs = jnp.concatenate([v1[b], cn_run[h].astype(BF16)], axis=0)
        nd.append(_dot(lhs[rows_of(b)], rhs))
        decay = jnp.exp(g_c[b] + m_prev[b] - m_new[b])
        cn_run[h] = jnp.concatenate([decay, decay], axis=1) * cn_run[h] + upd[b]
    nd = stack(nd)
    num = nd[:, :Dh]
    den = nd[:, Dh:]
    r = 1.0 / jnp.maximum(jnp.abs(den), jnp.exp(-m_t))
    z = stack([o_ref[cs(ci), hs(h)] for ci, h in inst]).astype(F32) * num
    zz_hi, zz_lo, _ = _split3(z * z)
    mz = (_dot(zz_hi, ones_blk) + _dot(zz_lo, ones_blk)) * (1.0 / Dh)
    hh = z * (r * lax.rsqrt(r * r * mz + EPS))
    for b, (ci, h) in enumerate(inst):
        y_ref[cs(ci), hs(h)] = (hh[rows_of(b)] * ng_ref[:, hs(h)]).astype(BF16)

    for h in range(M_HEADS):
        cn_ref[h] = cn_run[h]
        m_ref[h] = jnp.broadcast_to(m_run[h], m_ref.shape[1:])


def _place_left(onehot, x):
    hi, mid, lo = _split3(x)
    return _dot(onehot, hi) + _dot(onehot, mid) + _dot(onehot, lo)


def _mlstm(qk, v, o, gif, conv_w, conv_b, bif, norm_g, n_batch):
    n, w = v.shape
    rows = M_STEP_CHUNKS * M_CHUNK
    nc = n // n_batch // rows
    dh = w // M_HEADS
    row = lambda c: pl.BlockSpec((rows, c), lambda b, i: (b * nc + i, 0))
    return pl.pallas_call(
        _mlstm_kernel,
        grid=(n_batch, nc),
        in_specs=[row(2 * w), row(w), row(w), row(gif.shape[1]), _full(conv_w.shape),
                  _full(conv_b.shape), _full(bif.shape), _full(norm_g.shape)],
        out_specs=row(w),
        out_shape=jax.ShapeDtypeStruct((n, w), BF16),
        scratch_shapes=[pltpu.VMEM((2 * w // LANES // 2, 2 * (M_PAD + rows), LANES), F32),
                        pltpu.VMEM((M_HEADS, dh, 2 * dh), F32),
                        pltpu.VMEM((M_HEADS, 8, LANES), F32)],
        compiler_params=_params(("arbitrary", "arbitrary")),
        name="mlstm",
    )(qk, v, o, gif, conv_w, conv_b, bif, norm_g)


def _mem_kv_kernel(mem_ref, g_ref, wkv_ref, wq_ref, wo_ref, qk_ref, vo_ref):
    d = mem_ref.shape[2]
    n_mem = mem_ref.shape[1]
    dh = d // X_HEADS
    kv = _dot(_rmsnorm(mem_ref[0], g_ref[...]).astype(BF16), wkv_ref[...]).astype(BF16)
    for h in range(X_HEADS):
        k_h = kv[:, h * dh:(h + 1) * dh]
        v_h = kv[:, d + h * dh:d + (h + 1) * dh]
        qk_ref[0, :, h * n_mem:(h + 1) * n_mem] = (
            _dot_nt(wq_ref[:, h * dh:(h + 1) * dh], k_h) * (dh ** -0.5)).astype(BF16)
        vo_ref[0, h * n_mem:(h + 1) * n_mem, :] = _dot(v_h, wo_ref[h * dh:(h + 1) * dh, :]).astype(BF16)


def _mem_kv(mem, g, wkv, wq, wo):
    nb, n_mem, d = mem.shape
    hm = X_HEADS * n_mem
    return pl.pallas_call(
        _mem_kv_kernel,
        grid=(nb,),
        in_specs=[pl.BlockSpec((1, n_mem, d), lambda b: (b, 0, 0)), _full(g.shape), _full(wkv.shape),
                  _full(wq.shape), _full(wo.shape)],
        out_specs=[pl.BlockSpec((1, d, hm), lambda b: (b, 0, 0)),
                   pl.BlockSpec((1, hm, d), lambda b: (b, 0, 0))],
        out_shape=[jax.ShapeDtypeStruct((nb, d, hm), BF16), jax.ShapeDtypeStruct((nb, hm, d), BF16)],
        compiler_params=_params(("arbitrary",)),
        name="mem_kv",
    )(mem, g, wkv, wq, wo)


def _merge_attn_kernel(x_ref, yt_ref, ym_ref, gt_ref, wglu_ref, bglu_ref, wbs_ref, wbm_ref,
                       wout_ref, xg_ref, qkf_ref, vof_ref, x2_ref, ys_ref):
    d = x_ref.shape[1]
    rows = yt_ref.shape[1]
    n_slab = yt_ref.shape[2] // LANES
    p_m = _dot(ym_ref[...], wbm_ref[...])
    for t in range(S5_CHUNK):
        for j in range(n_slab):
            ys_ref[j, pl.ds(t, rows, stride=S5_CHUNK), :] = (
                yt_ref[t, :, j * LANES:(j + 1) * LANES].astype(F32))
    ys = _gelu(jnp.concatenate([ys_ref[j] for j in range(n_slab)], axis=1))
    ys = ys * _sigmoid(_dot(ys.astype(BF16), wglu_ref[...]) + bglu_ref[...])
    p_s = _dot(ys.astype(BF16), wbs_ref[...])
    merged = gt_ref[:, :d].astype(F32) * p_s + gt_ref[:, d:].astype(F32) * p_m
    x1 = x_ref[...] + _dot(merged.astype(BF16), wout_ref[...])

    s_all = _dot(_rmsnorm(x1, xg_ref[...]).astype(BF16), qkf_ref[0])
    n_mem = s_all.shape[1] // X_HEADS
    probs = []
    for h in range(X_HEADS):
        s = s_all[:, h * n_mem:(h + 1) * n_mem]
        p = jnp.exp(s - jnp.max(s, axis=-1, keepdims=True))
        probs.append((p / jnp.sum(p, axis=-1, keepdims=True)).astype(BF16))
    x2_ref[...] = x1 + _dot(jnp.concatenate(probs, axis=1), vof_ref[0])


def _merge_attn(x2d, yt, ym, gt, wglu, bglu, wbs, wbm, wout, xg, qkf, vof, n_batch):
    n, d = x2d.shape
    tm = PROJ_TILE
    per_b = n // n_batch // tm
    row = lambda c: pl.BlockSpec((tm, c), lambda b, i: (b * per_b + i, 0))
    per_batch = lambda a: pl.BlockSpec((1,) + a.shape[1:], lambda b, i: (b, 0, 0))
    T, _, s5w = yt.shape
    return pl.pallas_call(
        _merge_attn_kernel,
        grid=(n_batch, per_b),
        in_specs=[row(d), pl.BlockSpec((T, tm // T, s5w), lambda b, i: (0, b * per_b + i, 0)),
                  row(ym.shape[1]), row(gt.shape[1]),
                  _full(wglu.shape), _full(bglu.shape), _full(wbs.shape), _full(wbm.shape),
                  _full(wout.shape), _full(xg.shape), per_batch(qkf), per_batch(vof)],
        out_specs=row(d),
        out_shape=jax.ShapeDtypeStruct((n, d), F32),
        scratch_shapes=[pltpu.VMEM((s5w // LANES, tm, LANES), F32)],
        compiler_params=_params(("parallel", "parallel")),
        name="merge_attn",
    )(x2d, yt, ym, gt, wglu, bglu, wbs, wbm, wout, xg, qkf, vof)


def _conv_ffn_kernel(x_ref, fg_ref, wup_ref, cw_ref, cb_ref, wdn_ref, ng_ref, o_ref,
                     up_ref, act_ref, *, final_norm):
    tm = x_ref.shape[0]
    dff = wdn_ref.shape[0]
    ck = FFN_CHUNK
    n_chunks = dff // ck
    spc = ck // LANES
    base = 2 * FFN_PAD

    @pl.when(pl.program_id(1) == 0)
    def _():
        up_ref[:, 0:base, :] = jnp.zeros((up_ref.shape[0], base, LANES), F32)

    @pl.when(pl.program_id(1) > 0)
    def _():
        up_ref[:, 0:base, :] = up_ref[:, 2 * tm:2 * tm + base, :]

    x2 = x_ref[...]
    hb = _rmsnorm(x2, fg_ref[...]).astype(BF16)

    def up(c, half):
        c0 = half * dff + c * ck
        u = _dot(hb, wup_ref[:, c0:c0 + ck])
        for s in range(spc):
            up_ref[c * spc + s, pl.ds(base + half, tm, stride=2), :] = u[:, s * LANES:(s + 1) * LANES]

    def conv(c, half):
        c0 = half * dff + c * ck
        cols = []
        for s in range(spc):
            l0 = c0 + s * LANES
            acc = cb_ref[:, l0:l0 + LANES]
            for k in range(FFN_CONV):
                start = base + half - 2 * (FFN_CONV - 1 - k)
                acc = acc + (cw_ref[k:k + 1, l0:l0 + LANES]
                             * up_ref[c * spc + s, pl.ds(start, tm, stride=2), :])
            cols.append(acc)
        return jnp.concatenate(cols, axis=1)

    up(0, 0)
    up(0, 1)
    for c in range(n_chunks):
        if c + 1 < n_chunks:
            up(c + 1, 0)
            up(c + 1, 1)
        a = conv(c, 0)
        ab = a * conv(c, 1)
        th = jnp.tanh(a * (GELU_C1 + GELU_C3 * (a * a)))
        act_ref[:, c * ck:(c + 1) * ck] = (ab + ab * th).astype(BF16)
    y = x2 + _dot(act_ref[...], wdn_ref[...])
    o_ref[...] = _rmsnorm(y, ng_ref[...]) if final_norm else y


def _conv_ffn(x2, fg, wup, cw, cb, wdn, ng, n_batch, final_norm):
    n, d = x2.shape
    tm = TOKEN_TILE
    per_b = n // n_batch // tm
    row = pl.BlockSpec((tm, d), lambda b, i: (b * per_b + i, 0))
    return pl.pallas_call(
        functools.partial(_conv_ffn_kernel, final_norm=final_norm),
        grid=(n_batch, per_b),
        in_specs=[row, _full(fg.shape), _full(wup.shape), _full(cw.shape), _full(cb.shape),
                  _full(wdn.shape), _full(ng.shape)],
        out_specs=row,
        out_shape=jax.ShapeDtypeStruct((n, d), F32),
        scratch_shapes=[pltpu.VMEM((wdn.shape[0] // LANES, 2 * (FFN_PAD + tm), LANES), F32),
                        pltpu.VMEM((tm, wdn.shape[0]), BF16)],
        compiler_params=_params(("arbitrary", "arbitrary")),
        name="conv_ffn",
    )(x2, fg, wup, cw, cb, wdn, ng)


def kernel(x, mem, mix_norm_g, w_in, s5_lam_re, s5_lam_im, s5_b_re, s5_b_im, s5_c_re, s5_c_im,
           s5_d, s5_log_dt, s5_w_glu, s5_b_glu, m_conv_w, m_conv_b, m_b_i, m_b_f, m_norm_g,
           w_br_s5, w_br_m, b_gate, w_out, x_norm_g, mem_norm_g, x_wq, x_wkv, x_wo,
           f_norm_g, f_w_up, f_conv_w, f_conv_b, f_w_down, final_norm_g):
    B, L, D = x.shape
    N = B * L
    depth = w_in.shape[0]
    G = s5_lam_re.shape[1]
    s5w = G * S5_GROUP
    mw = (w_in.shape[2] - s5w - 2 * M_HEADS - 2 * D) // 4
    row = lambda a: a.reshape(1, -1)
    bf = lambda a: a.astype(BF16)

    xs = x.reshape(N, D)
    for l in range(depth):
        wl = w_in[l]
        c0 = s5w
        c1 = c0 + 2 * mw
        c2 = c1 + mw
        c3 = c2 + mw
        c4 = c3 + 2 * M_HEADS
        wif = jnp.pad(wl[:, c3:c4], ((0, 0), (0, LANES - 2 * M_HEADS)))
        ut, qk, v, o, gif, gates = _in_proj(
            xs, row(mix_norm_g[l]), bf(wl[:, :c0]), bf(wl[:, c0:c1]), bf(wl[:, c1:c2]),
            bf(wl[:, c2:c3]), bf(wif), bf(wl[:, c4:]), row(b_gate[l]))

        bd, w_st, v_st, a16 = _s5_prep(
            s5_log_dt[l], s5_lam_re[l], s5_lam_im[l], s5_b_re[l], s5_b_im[l], s5_c_re[l],
            s5_c_im[l], s5_d[l])
        yt = _s5_scan(ut, bd, w_st, v_st, a16, B)

        bif = jnp.pad(jnp.concatenate([m_b_i[l], m_b_f[l]]), (0, LANES - 2 * M_HEADS)).reshape(1, LANES)
        y_m = _mlstm(qk, v, o, gif, m_conv_w[l], row(m_conv_b[l]), bif, row(m_norm_g[l]), B)

        qkf, vof = _mem_kv(mem, row(mem_norm_g[l]), bf(x_wkv[l]), bf(x_wq[l]), bf(x_wo[l]))
        x2 = _merge_attn(xs, yt, y_m, gates, bf(s5_w_glu[l]), row(s5_b_glu[l]), bf(w_br_s5[l]),
                         bf(w_br_m[l]), bf(w_out[l]), row(x_norm_g[l]), qkf, vof, B)
        dff = f_w_down.shape[1]
        half = jnp.concatenate([jnp.ones((dff,), F32), jnp.full((dff,), 0.5, F32)])
        xs = _conv_ffn(x2, row(f_norm_g[l]), bf(f_w_up[l]), f_conv_w[l] * half,
                       row(f_conv_b[l] * half), bf(f_w_down[l]), row(final_norm_g), B,
                       l == depth - 1)
    return xs.reshape(B, L, D)
```

```python
import functools
import math

import jax
import jax.numpy as jnp
from jax import lax
from jax.experimental import pallas as pl
from jax.experimental.pallas import tpu as pltpu

F32 = jnp.float32
BF16 = jnp.bfloat16

EPS = 1e-6
S5_GROUP = 16
S5_STATE = 64
S5_CHUNK = 16
M_HEADS = 4
M_CONV = 4
M_CHUNK = 128
M_STEP_CHUNKS = 8
M_PAD = 8
X_HEADS = 4
FFN_CONV = 3
NEG_BIG = -1e30

LANES = 128
S5_BLOCK = LANES // S5_GROUP
TOKEN_TILE = 512
PROJ_TILE = 1024
FFN_CHUNK = 256
FFN_PAD = 8
VMEM_LIMIT = 56 * 1024 * 1024


def _dot(a, b):
    return jnp.dot(a, b, preferred_element_type=F32)


def _dot_nt(a, b):
    return lax.dot_general(a, b, (((1,), (1,)), ((), ())), preferred_element_type=F32)


def _sigmoid(x):
    return 1.0 / (1.0 + jnp.exp2(x * (-1.0 / math.log(2.0))))


def _log_sigmoid(x):
    return jnp.minimum(x, 0.0) - jnp.log(1.0 + jnp.exp(-jnp.abs(x)))


GELU_C1 = math.sqrt(2.0 / math.pi)
GELU_C3 = 0.044715 * GELU_C1


def _gelu(x):
    return x * (0.5 * (1.0 + jnp.tanh(x * (GELU_C1 + GELU_C3 * (x * x)))))


def _rmsnorm(x, g):
    return x * lax.rsqrt(jnp.mean(x * x, axis=-1, keepdims=True) + EPS) * g


def _cmul(ar, ai, br, bi):
    return ar * br - ai * bi, ar * bi + ai * br


def _full(shape):
    n = len(shape)
    return pl.BlockSpec(shape, lambda *_: (0,) * n)


def _params(sem):
    return pltpu.CompilerParams(dimension_semantics=sem, vmem_limit_bytes=VMEM_LIMIT)


def _in_proj_kernel(x_ref, g_ref, w_ref, wg_ref, bg_ref,
                    ut_ref, qk_ref, v_ref, o_ref, if_ref, gt_ref, us_ref, *, widths):
    c_u, c_qk, c_v, c_o = widths
    o1, o2, o3, o4 = c_u, c_u + c_qk, c_u + c_qk + c_v, c_u + c_qk + c_v + c_o
    hb = _rmsnorm(x_ref[...], g_ref[...]).astype(BF16)
    u = _dot(hb, w_ref[:, 0:o1])
    n_slab = u.shape[1] // LANES
    for j in range(n_slab):
        us_ref[j] = u[:, j * LANES:(j + 1) * LANES]
    rows = u.shape[0] // S5_CHUNK
    for t in range(S5_CHUNK):
        for j in range(n_slab):
            ut_ref[t, :, j * LANES:(j + 1) * LANES] = (
                us_ref[j, pl.ds(t, rows, stride=S5_CHUNK), :].astype(BF16))
    qk_ref[...] = _dot(hb, w_ref[:, o1:o2]).astype(BF16)
    v_ref[...] = _dot(hb, w_ref[:, o2:o3]).astype(BF16)
    o_ref[...] = _sigmoid(_dot(hb, w_ref[:, o3:o4])).astype(BF16)
    if_ref[...] = _dot(hb, w_ref[:, o4:o4 + LANES])
    gt_ref[...] = _sigmoid(_dot(hb, wg_ref[...]) + bg_ref[...]).astype(BF16)


def _in_proj(x2d, g, w_all, wg, bg, widths):
    n, d = x2d.shape
    tm = PROJ_TILE
    assert all(c % LANES == 0 for c in widths) and sum(widths) + LANES <= w_all.shape[1]
    row = lambda c: pl.BlockSpec((tm, c), lambda i: (i, 0))
    s5w, c_qk, c_v, c_o = widths
    outs = [(c_qk, BF16), (c_v, BF16), (c_o, BF16), (LANES, F32), (wg.shape[1], BF16)]
    T = S5_CHUNK
    return pl.pallas_call(
        functools.partial(_in_proj_kernel, widths=widths),
        grid=(n // tm,),
        in_specs=[row(d), _full(g.shape), _full(w_all.shape), _full(wg.shape), _full(bg.shape)],
        out_specs=[pl.BlockSpec((T, tm // T, s5w), lambda i: (0, i, 0))] + [row(c) for c, _ in outs],
        out_shape=([jax.ShapeDtypeStruct((T, n // T, s5w), BF16)]
                   + [jax.ShapeDtypeStruct((n, c), dt) for c, dt in outs]),
        scratch_shapes=[pltpu.VMEM((s5w // LANES, tm, LANES), F32)],
        compiler_params=_params(("parallel",)),
        name="in_proj",
    )(x2d, g, w_all, wg, bg)


def _split3(x):
    hi = x.astype(BF16)
    r1 = x - hi.astype(F32)
    mid = r1.astype(BF16)
    lo = (r1 - mid.astype(F32)).astype(BF16)
    return hi, mid, lo


def _dot_x3(a, b):
    a_hi, a_lo, _ = _split3(a)
    b_hi, b_lo, _ = _split3(b)
    return _dot(a_hi, b_hi) + _dot(a_hi, b_lo) + _dot(a_lo, b_hi)


def _place(x, onehot):
    hi, mid, lo = _split3(x)
    return _dot(hi, onehot) + _dot(mid, onehot) + _dot(lo, onehot)


def _s5_prep_kernel(ldt_r_ref, lre_r_ref, lim_r_ref, ldt_c_ref, lre_c_ref, lim_c_ref,
                    btre_ref, btim_ref, ctre_ref, ctim_ref, d_ref,
                    bd_ref, w_ref, v_ref, a16_ref, pre_ref, pim_ref):
    P, H, T, M = S5_STATE, S5_GROUP, S5_CHUNK, S5_BLOCK
    MP, MH = M * P, M * H

    lr, li = lre_r_ref[0], lim_r_ref[0]
    dt_r = jnp.exp(ldt_r_ref[0])
    e_r = jnp.exp(lr * dt_r)
    ar_r, ai_r = e_r * jnp.cos(li * dt_r), e_r * jnp.sin(li * dt_r)
    dt_c = jnp.exp(ldt_c_ref[0])
    e_c = jnp.exp(lre_c_ref[0] * dt_c)
    ar_c, ai_c = e_c * jnp.cos(lim_c_ref[0] * dt_c), e_c * jnp.sin(lim_c_ref[0] * dt_c)

    t_r = lax.broadcasted_iota(jnp.int32, (P, MP), 0)
    t_l = lax.broadcasted_iota(jnp.int32, (P, MP), 1)
    tile_p = (jnp.bitwise_and(t_l, P - 1) == t_r).astype(BF16)
    b_r = lax.broadcasted_iota(jnp.int32, (MH, MP), 0)
    b_l = lax.broadcasted_iota(jnp.int32, (MH, MP), 1)
    sh_h, sh_p = H.bit_length() - 1, P.bit_length() - 1
    own_b = jnp.right_shift(b_r, sh_h) == jnp.right_shift(b_l, sh_p)
    bre = jnp.where(own_b, _place(btre_ref[0], tile_p), 0.0)
    bim = jnp.where(own_b, _place(btim_ref[0], tile_p), 0.0)
    inv = 1.0 / (lr * lr + li * li)
    cr = ((ar_r - 1.0) * lr + ai_r * li) * inv
    ci = (ai_r * lr - (ar_r - 1.0) * li) * inv
    bre, bim = _cmul(cr, ci, bre, bim)

    u_r = lax.broadcasted_iota(jnp.int32, (H, MH), 0)
    u_l = lax.broadcasted_iota(jnp.int32, (H, MH), 1)
    tile_h = (jnp.bitwise_and(u_l, H - 1) == u_r).astype(BF16)
    c_r = lax.broadcasted_iota(jnp.int32, (MP, MH), 0)
    c_l = lax.broadcasted_iota(jnp.int32, (MP, MH), 1)
    own_c = jnp.right_shift(c_r, sh_p) == jnp.right_shift(c_l, sh_h)
    cre = jnp.where(own_c, _place(ctre_ref[0], tile_h), 0.0)
    cim = jnp.where(own_c, _place(ctim_ref[0], tile_h), 0.0)

    qre, qim = cre, cim
    for tau in range(T + 1):
        pre_ref[:, tau * MH:(tau + 1) * MH] = qre
        pim_ref[:, tau * MH:(tau + 1) * MH] = qim
        if tau < T:
            qre, qim = _cmul(ar_c, ai_c, qre, qim)

    bd = _dot_x3(bre, pre_ref[:, 0:T * MH]) - _dot_x3(bim, pim_ref[:, 0:T * MH])
    d_r = lax.broadcasted_iota(jnp.int32, (MH, MH), 0)
    d_l = lax.broadcasted_iota(jnp.int32, (MH, MH), 1)
    bd_ref[0, :, 0:MH] = (bd[:, 0:MH] + jnp.where(d_r == d_l, d_ref[0], 0.0)).astype(BF16)
    bd_ref[0, :, MH:T * MH] = bd[:, MH:T * MH].astype(BF16)

    v_ref[0, 0:MP, :] = pre_ref[:, MH:(T + 1) * MH].astype(BF16)
    v_ref[0, MP:2 * MP, :] = (-pim_ref[:, MH:(T + 1) * MH]).astype(BF16)

    wre, wim = bre, bim
    for t in range(T - 1, -1, -1):
        w_ref[0, t * MH:(t + 1) * MH, 0:MP] = wre.astype(BF16)
        w_ref[0, t * MH:(t + 1) * MH, MP:2 * MP] = wim.astype(BF16)
        if t > 0:
            wre, wim = _cmul(ar_r, ai_r, wre, wim)

    sr, si = ar_r, ai_r
    for _ in range(4):
        sr, si = _cmul(sr, si, sr, si)
    row8 = lax.broadcasted_iota(jnp.int32, (8, MP), 0)
    a16_ref[0] = jnp.where(row8 == 0, sr, si)


def _s5_prep(log_dt, lam_re, lam_im, b_re, b_im, c_re, c_im, d):
    G, P = lam_re.shape
    H, T, M = S5_GROUP, S5_CHUNK, S5_BLOCK
    J = G // M
    MP, MH = M * P, M * H
    ldt = jnp.repeat(log_dt, P)
    args = [
        ldt.reshape(J, 1, MP), lam_re.reshape(J, 1, MP), lam_im.reshape(J, 1, MP),
        ldt.reshape(J, MP, 1), lam_re.reshape(J, MP, 1), lam_im.reshape(J, MP, 1),
        b_re.transpose(0, 2, 1).reshape(J, MH, P), b_im.transpose(0, 2, 1).reshape(J, MH, P),
        c_re.transpose(0, 2, 1).reshape(J, MP, H), c_im.transpose(0, 2, 1).reshape(J, MP, H),
        d.reshape(J, 1, MH),
    ]
    blk = lambda s: pl.BlockSpec((1,) + tuple(s[1:]), lambda j: (j, 0, 0))
    outs = [((J, MH, T * MH), BF16), ((J, T * MH, 2 * MP), BF16), ((J, 2 * MP, T * MH), BF16),
            ((J, 8, MP), F32)]
    return pl.pallas_call(
        _s5_prep_kernel,
        grid=(J,),
        in_specs=[blk(a.shape) for a in args],
        out_specs=[blk(s) for s, _ in outs],
        out_shape=[jax.ShapeDtypeStruct(s, dt) for s, dt in outs],
        scratch_shapes=[pltpu.VMEM((MP, (T + 1) * MH), F32)] * 2,
        compiler_params=_params(("parallel",)),
        name="s5_prep",
    )(*args)


def _s5_scan_kernel(ut_ref, bd_ref, w_ref, v_ref, a16_ref, yt_ref, big_ref, xs_ref, sp_ref,
                    part_ref):
    T = S5_CHUNK
    R = ut_ref.shape[1]
    MH = ut_ref.shape[2]
    MP = a16_ref.shape[2]

    @pl.when(pl.program_id(1) == 0)
    def _():
        for s in range(T):
            for t in range(T):
                blk = (bd_ref[0, :, (t - s) * MH:(t - s + 1) * MH] if t >= s
                       else jnp.zeros((MH, MH), BF16))
                big_ref[s * MH:(s + 1) * MH, t * MH:(t + 1) * MH] = blk

    x = jnp.concatenate([ut_ref[t] for t in range(T)], axis=1)
    xs_ref[...] = _dot(x, w_ref[0])
    a16 = a16_ref[0]
    a1 = (jnp.broadcast_to(a16[0:1], (8, MP)), jnp.broadcast_to(a16[1:2], (8, MP)))
    a2 = _cmul(*a1, *a1)
    a4 = _cmul(*a2, *a2)
    rid = lax.broadcasted_iota(jnp.int32, (8, MP), 0)
    pw = a1
    for r in range(1, 8):
        nxt = _cmul(*pw, *a1)
        pw = (jnp.where(rid >= r, nxt[0], pw[0]), jnp.where(rid >= r, nxt[1], pw[1]))

    def shift_rows(v, n):
        return jnp.where(rid >= n, pltpu.roll(v, n, 0), 0.0)

    def body(i, carry):
        sr, si = carry
        r0 = i * 8
        yr = xs_ref[r0:r0 + 8, 0:MP]
        yi = xs_ref[r0:r0 + 8, MP:2 * MP]
        for n, am in ((1, a1), (2, a2), (4, a4)):
            tr, ti = _cmul(*am, shift_rows(yr, n), shift_rows(yi, n))
            yr, yi = yr + tr, yi + ti
        cr, ci = _cmul(*pw, jnp.broadcast_to(sr, (8, MP)), jnp.broadcast_to(si, (8, MP)))
        er, ei = yr + cr, yi + ci
        sp_ref[r0:r0 + 8, 0:MP] = jnp.where(rid == 0, sr, pltpu.roll(er, 1, 0))
        sp_ref[r0:r0 + 8, MP:2 * MP] = jnp.where(rid == 0, si, pltpu.roll(ei, 1, 0))
        return er[7:8], ei[7:8]

    for p in range(T // 2):
        k = 2 * (p + 1) * MH
        cols = slice(2 * p * MH, 2 * (p + 1) * MH)
        part_ref[:, cols] = _dot(x[:, 0:k], big_ref[0:k, cols])

    carry = (jnp.zeros((1, MP), F32), jnp.zeros((1, MP), F32))
    for i in range(R // 8):
        carry = body(i, carry)

    sp = sp_ref[...].astype(BF16)
    for p in range(T // 2):
        cols = slice(2 * p * MH, 2 * (p + 1) * MH)
        acc = part_ref[:, cols] + _dot(sp, v_ref[0, :, cols])
        yt_ref[2 * p] = acc[:, 0:MH].astype(BF16)
        yt_ref[2 * p + 1] = acc[:, MH:2 * MH].astype(BF16)


def _s5_scan(ut, bd, w, v, a16, n_batch):
    T, nrow, s5w = ut.shape
    R = nrow // n_batch
    MH = LANES
    MP = a16.shape[2]
    per_j = lambda s: pl.BlockSpec((1,) + tuple(s[1:]), lambda j, b: (j, 0, 0))
    io = pl.BlockSpec((T, R, MH), lambda j, b: (0, b, j))
    return pl.pallas_call(
        _s5_scan_kernel,
        grid=(s5w // MH, n_batch),
        in_specs=[io, per_j(bd.shape), per_j(w.shape), per_j(v.shape), per_j(a16.shape)],
        out_specs=io,
        out_shape=jax.ShapeDtypeStruct(ut.shape, BF16),
        scratch_shapes=[pltpu.VMEM((T * MH, T * MH), BF16),
                        pltpu.VMEM((R, 2 * MP), F32),
                        pltpu.VMEM((R, 2 * MP), F32),
                        pltpu.VMEM((R, T * MH), F32)],
        compiler_params=_params(("arbitrary", "arbitrary")),
        name="s5_scan",
    )(ut, bd, w, v, a16)


def _mlstm_kernel(qk_ref, v_ref, o_ref, if_ref, cw_ref, cb_ref, bif_ref, ng_ref, y_ref,
                  cv_ref, cn_ref, m_ref):
    Lc = M_CHUNK
    S = qk_ref.shape[0]
    W = v_ref.shape[1]
    Dh = W // M_HEADS
    base = 2 * M_PAD

    @pl.when(pl.program_id(1) == 0)
    def _():
        cv_ref[:, 0:base, :] = jnp.zeros((cv_ref.shape[0], base, LANES), F32)
        cn_ref[...] = jnp.zeros(cn_ref.shape, F32)
        m_ref[...] = jnp.zeros(m_ref.shape, F32)

    @pl.when(pl.program_id(1) > 0)
    def _():
        cv_ref[:, 0:base, :] = cv_ref[:, 2 * S:2 * S + base, :]

    x = qk_ref[...].astype(F32)
    n_slab = x.shape[1] // LANES
    for s in range(n_slab):
        cv_ref[s // 2, pl.ds(base + s % 2, S, stride=2), :] = x[:, s * LANES:(s + 1) * LANES]
    cols = []
    for s in range(n_slab):
        l0 = s * LANES
        acc = cb_ref[:, l0:l0 + LANES]
        for k in range(M_CONV):
            start = base + s % 2 - 2 * (M_CONV - 1 - k)
            acc = acc + cw_ref[k:k + 1, l0:l0 + LANES] * cv_ref[s // 2, pl.ds(start, S, stride=2), :]
        cols.append(acc * _sigmoid(acc))
    qk = jnp.concatenate(cols, axis=1)

    n_ci = S // Lc
    inst = [(ci, h) for ci in range(n_ci) for h in range(M_HEADS)]
    rows_of = lambda b: slice(b * Lc, (b + 1) * Lc)
    stack = lambda parts: jnp.concatenate(parts, axis=0)
    rep = lambda parts: stack([jnp.broadcast_to(p, (Lc, Dh)) for p in parts])

    lane = lax.broadcasted_iota(jnp.int32, (Lc, Lc), 1)
    row = lax.broadcasted_iota(jnp.int32, (Lc, Lc), 0)
    causal1 = row >= lane
    tril = causal1.astype(BF16)
    ones_blk = jnp.ones((Lc, Dh), BF16)

    gm, bcu, gm_t, bcu_t = [], [], [], []
    for ci in range(n_ci):
        pre = if_ref[ci * Lc:(ci + 1) * Lc, :] + bif_ref[...]
        g = jnp.where(lane < M_HEADS, pre, jnp.where(lane < 2 * M_HEADS, _log_sigmoid(pre), 0.0))
        c = _place_left(tril, g)
        gm.append(g)
        bcu.append(c)
        gm_t.append(g.T)
        bcu_t.append(c.T)

    hs = lambda h: slice(h * Dh, (h + 1) * Dh)
    cs = lambda ci: slice(ci * Lc, (ci + 1) * Lc)
    q = stack([qk[cs(ci), hs(h)] for ci, h in inst])
    k = stack([qk[cs(ci), W + h * Dh:W + (h + 1) * Dh] for ci, h in inst]) * (Dh ** -0.5)
    v1 = [jnp.concatenate([v_ref[cs(ci), hs(h)], ones_blk], axis=1) for ci, h in inst]
    ig = rep([gm[ci][:, h:h + 1] for ci, h in inst])
    bc = rep([bcu[ci][:, M_HEADS + h:M_HEADS + h + 1] for ci, h in inst])
    g_c = [jnp.broadcast_to(bcu[ci][Lc - 1:Lc, M_HEADS + h:M_HEADS + h + 1], (1, Dh))
           for ci, h in inst]
    r_mat = stack([jnp.where(causal1, gm_t[ci][h:h + 1, :] - bcu_t[ci][M_HEADS + h:M_HEADS + h + 1, :],
                             NEG_BIG) for ci, h in inst])

    a_end = rep(g_c) - bc + ig
    m_loc = [jnp.max(a_end[rows_of(b)], axis=0, keepdims=True) for b in range(len(inst))]
    m_run = [m_ref[h][0:1, :] for h in range(M_HEADS)]
    m_prev, m_new = [], []
    for b, (ci, h) in enumerate(inst):
        m_prev.append(m_run[h])
        m_run[h] = jnp.maximum(g_c[b] + m_run[h], m_loc[b])
        m_new.append(m_run[h])
    w_rep = jnp.exp(a_end - rep(m_new))

    mp = rep(m_prev)
    mx = jnp.maximum(mp, jnp.broadcast_to(jnp.max(r_mat, axis=1, keepdims=True), mp.shape))
    m_t = bc + mx
    qb, kb = q.astype(BF16), k.astype(BF16)
    s_mat = stack([_dot_nt(qb[rows_of(b)], kb[rows_of(b)]) for b in range(len(inst))])
    s_mat = s_mat * jnp.exp(r_mat - mx)
    inter_w = jnp.exp(mp - mx)
    lhs = jnp.concatenate([s_mat.astype(BF16), (inter_w * q).astype(BF16)], axis=1)

    kw = k * w_rep
    upd = [_dot(kw[rows_of(b)].T.astype(BF16), v1[b]) for b in range(len(inst))]
    cn_run = [cn_ref[h] for h in range(M_HEADS)]
    nd = []
    for b, (ci, h) in enumerate(inst):
        rhs = jnp.concatenate([v1[b], cn_run[h].astype(BF16)], axis=0)
        nd.append(_dot(lhs[rows_of(b)], rhs))
        decay = jnp.exp(g_c[b] + m_prev[b] - m_new[b])
        cn_run[h] = jnp.concatenate([decay, decay], axis=1) * cn_run[h] + upd[b]
    nd = stack(nd)
    num = nd[:, :Dh]
    den = nd[:, Dh:]
    r = 1.0 / jnp.maximum(jnp.abs(den), jnp.exp(-m_t))
    z = stack([o_ref[cs(ci), hs(h)] for ci, h in inst]).astype(F32) * num
    zz_hi, zz_lo, _ = _split3(z * z)
    mz = (_dot(zz_hi, ones_blk) + _dot(zz_lo, ones_blk)) * (1.0 / Dh)
    hh = z * (r * lax.rsqrt(r * r * mz + EPS))
    for b, (ci, h) in enumerate(inst):
        y_ref[cs(ci), hs(h)] = (hh[rows_of(b)] * ng_ref[:, hs(h)]).astype(BF16)

    for h in range(M_HEADS):
        cn_ref[h] = cn_run[h]
        m_ref[h] = jnp.broadcast_to(m_run[h], m_ref.shape[1:])


def _place_left(onehot, x):
    hi, mid, lo = _split3(x)
    return _dot(onehot, hi) + _dot(onehot, mid) + _dot(onehot, lo)


def _mlstm(qk, v, o, gif, conv_w, conv_b, bif, norm_g, n_batch):
    n, w = v.shape
    rows = M_STEP_CHUNKS * M_CHUNK
    nc = n // n_batch // rows
    dh = w // M_HEADS
    row = lambda c: pl.BlockSpec((rows, c), lambda b, i: (b * nc + i, 0))
    return pl.pallas_call(
        _mlstm_kernel,
        grid=(n_batch, nc),
        in_specs=[row(2 * w), row(w), row(w), row(gif.shape[1]), _full(conv_w.shape),
                  _full(conv_b.shape), _full(bif.shape), _full(norm_g.shape)],
        out_specs=row(w),
        out_shape=jax.ShapeDtypeStruct((n, w), BF16),
        scratch_shapes=[pltpu.VMEM((2 * w // LANES // 2, 2 * (M_PAD + rows), LANES), F32),
                        pltpu.VMEM((M_HEADS, dh, 2 * dh), F32),
                        pltpu.VMEM((M_HEADS, 8, LANES), F32)],
        compiler_params=_params(("arbitrary", "arbitrary")),
        name="mlstm",
    )(qk, v, o, gif, conv_w, conv_b, bif, norm_g)


def _mem_kv_kernel(mem_ref, g_ref, wkv_ref, wq_ref, wo_ref, qk_ref, vo_ref):
    d = mem_ref.shape[2]
    n_mem = mem_ref.shape[1]
    dh = d // X_HEADS
    kv = _dot(_rmsnorm(mem_ref[0], g_ref[...]).astype(BF16), wkv_ref[...]).astype(BF16)
    for h in range(X_HEADS):
        k_h = kv[:, h * dh:(h + 1) * dh]
        v_h = kv[:, d + h * dh:d + (h + 1) * dh]
        qk_ref[0, :, h * n_mem:(h + 1) * n_mem] = (
            _dot_nt(wq_ref[:, h * dh:(h + 1) * dh], k_h) * (dh ** -0.5)).astype(BF16)
        vo_ref[0, h * n_mem:(h + 1) * n_mem, :] = _dot(v_h, wo_ref[h * dh:(h + 1) * dh, :]).astype(BF16)


def _mem_kv(mem, g, wkv, wq, wo):
    nb, n_mem, d = mem.shape
    hm = X_HEADS * n_mem
    return pl.pallas_call(
        _mem_kv_kernel,
        grid=(nb,),
        in_specs=[pl.BlockSpec((1, n_mem, d), lambda b: (b, 0, 0)), _full(g.shape), _full(wkv.shape),
                  _full(wq.shape), _full(wo.shape)],
        out_specs=[pl.BlockSpec((1, d, hm), lambda b: (b, 0, 0)),
                   pl.BlockSpec((1, hm, d), lambda b: (b, 0, 0))],
        out_shape=[jax.ShapeDtypeStruct((nb, d, hm), BF16), jax.ShapeDtypeStruct((nb, hm, d), BF16)],
        compiler_params=_params(("arbitrary",)),
        name="mem_kv",
    )(mem, g, wkv, wq, wo)


def _merge_attn_kernel(x_ref, yt_ref, ym_ref, gt_ref, wglu_ref, bglu_ref, wbs_ref, wbm_ref,
                       wout_ref, xg_ref, qkf_ref, vof_ref, x2_ref, ys_ref):
    d = x_ref.shape[1]
    rows = yt_ref.shape[1]
    n_slab = yt_ref.shape[2] // LANES
    p_m = _dot(ym_ref[...], wbm_ref[...])
    for t in range(S5_CHUNK):
        for j in range(n_slab):
            ys_ref[j, pl.ds(t, rows, stride=S5_CHUNK), :] = (
                yt_ref[t, :, j * LANES:(j + 1) * LANES].astype(F32))
    ys = _gelu(jnp.concatenate([ys_ref[j] for j in range(n_slab)], axis=1))
    ys = ys * _sigmoid(_dot(ys.astype(BF16), wglu_ref[...]) + bglu_ref[...])
    p_s = _dot(ys.astype(BF16), wbs_ref[...])
    merged = gt_ref[:, :d].astype(F32) * p_s + gt_ref[:, d:].astype(F32) * p_m
    x1 = x_ref[...] + _dot(merged.astype(BF16), wout_ref[...])

    s_all = _dot(_rmsnorm(x1, xg_ref[...]).astype(BF16), qkf_ref[0])
    n_mem = s_all.shape[1] // X_HEADS
    probs = []
    for h in range(X_HEADS):
        s = s_all[:, h * n_mem:(h + 1) * n_mem]
        p = jnp.exp(s - jnp.max(s, axis=-1, keepdims=True))
        probs.append((p / jnp.sum(p, axis=-1, keepdims=True)).astype(BF16))
    x2_ref[...] = x1 + _dot(jnp.concatenate(probs, axis=1), vof_ref[0])


def _merge_attn(x2d, yt, ym, gt, wglu, bglu, wbs, wbm, wout, xg, qkf, vof, n_batch):
    n, d = x2d.shape
    tm = PROJ_TILE
    per_b = n // n_batch // tm
    row = lambda c: pl.BlockSpec((tm, c), lambda b, i: (b * per_b + i, 0))
    per_batch = lambda a: pl.BlockSpec((1,) + a.shape[1:], lambda b, i: (b, 0, 0))
    T, _, s5w = yt.shape
    return pl.pallas_call(
        _merge_attn_kernel,
        grid=(n_batch, per_b),
        in_specs=[row(d), pl.BlockSpec((T, tm // T, s5w), lambda b, i: (0, b * per_b + i, 0)),
                  row(ym.shape[1]), row(gt.shape[1]),
                  _full(wglu.shape), _full(bglu.shape), _full(wbs.shape), _full(wbm.shape),
                  _full(wout.shape), _full(xg.shape), per_batch(qkf), per_batch(vof)],
        out_specs=row(d),
        out_shape=jax.ShapeDtypeStruct((n, d), F32),
        scratch_shapes=[pltpu.VMEM((s5w // LANES, tm, LANES), F32)],
        compiler_params=_params(("parallel", "parallel")),
        name="merge_attn",
    )(x2d, yt, ym, gt, wglu, bglu, wbs, wbm, wout, xg, qkf, vof)


def _conv_ffn_kernel(x_ref, fg_ref, wup_ref, cw_ref, cb_ref, wdn_ref, ng_ref, o_ref,
                     up_ref, act_ref, *, final_norm):
    tm = x_ref.shape[0]
    dff = wdn_ref.shape[0]
    ck = FFN_CHUNK
    n_chunks = dff // ck
    spc = ck // LANES
    base = 2 * FFN_PAD

    @pl.when(pl.program_id(1) == 0)
    def _():
        up_ref[:, 0:base, :] = jnp.zeros((up_ref.shape[0], base, LANES), F32)

    @pl.when(pl.program_id(1) > 0)
    def _():
        up_ref[:, 0:base, :] = up_ref[:, 2 * tm:2 * tm + base, :]

    x2 = x_ref[...]
    hb = _rmsnorm(x2, fg_ref[...]).astype(BF16)

    def up(c, half):
        c0 = half * dff + c * ck
        u = _dot(hb, wup_ref[:, c0:c0 + ck])
        for s in range(spc):
            up_ref[c * spc + s, pl.ds(base + half, tm, stride=2), :] = u[:, s * LANES:(s + 1) * LANES]

    def conv(c, half):
        c0 = half * dff + c * ck
        cols = []
        for s in range(spc):
            l0 = c0 + s * LANES
            acc = cb_ref[:, l0:l0 + LANES]
            for k in range(FFN_CONV):
                start = base + half - 2 * (FFN_CONV - 1 - k)
                acc = acc + (cw_ref[k:k + 1, l0:l0 + LANES]
                             * up_ref[c * spc + s, pl.ds(start, tm, stride=2), :])
            cols.append(acc)
        return jnp.concatenate(cols, axis=1)

    up(0, 0)
    up(0, 1)
    for c in range(n_chunks):
        if c + 1 < n_chunks:
            up(c + 1, 0)
            up(c + 1, 1)
        a = conv(c, 0)
        ab = a * conv(c, 1)
        th = jnp.tanh(a * (GELU_C1 + GELU_C3 * (a * a)))
        act_ref[:, c * ck:(c + 1) * ck] = (ab + ab * th).astype(BF16)
    y = x2 + _dot(act_ref[...], wdn_ref[...])
    o_ref[...] = _rmsnorm(y, ng_ref[...]) if final_norm else y


def _conv_ffn(x2, fg, wup, cw, cb, wdn, ng, n_batch, final_norm):
    n, d = x2.shape
    tm = TOKEN_TILE
    per_b = n // n_batch // tm
    row = pl.BlockSpec((tm, d), lambda b, i: (b * per_b + i, 0))
    return pl.pallas_call(
        functools.partial(_conv_ffn_kernel, final_norm=final_norm),
        grid=(n_batch, per_b),
        in_specs=[row, _full(fg.shape), _full(wup.shape), _full(cw.shape), _full(cb.shape),
                  _full(wdn.shape), _full(ng.shape)],
        out_specs=row,
        out_shape=jax.ShapeDtypeStruct((n, d), F32),
        scratch_shapes=[pltpu.VMEM((wdn.shape[0] // LANES, 2 * (FFN_PAD + tm), LANES), F32),
                        pltpu.VMEM((tm, wdn.shape[0]), BF16)],
        compiler_params=_params(("arbitrary", "arbitrary")),
        name="conv_ffn",
    )(x2, fg, wup, cw, cb, wdn, ng)


def kernel(x, mem, mix_norm_g, w_in, s5_lam_re, s5_lam_im, s5_b_re, s5_b_im, s5_c_re, s5_c_im,
           s5_d, s5_log_dt, s5_w_glu, s5_b_glu, m_conv_w, m_conv_b, m_b_i, m_b_f, m_norm_g,
           w_br_s5, w_br_m, b_gate, w_out, x_norm_g, mem_norm_g, x_wq, x_wkv, x_wo,
           f_norm_g, f_w_up, f_conv_w, f_conv_b, f_w_down, final_norm_g):
    B, L, D = x.shape
    N = B * L
    depth = w_in.shape[0]
    G = s5_lam_re.shape[1]
    s5w = G * S5_GROUP
    mw = (w_in.shape[2] - s5w - 2 * M_HEADS - 2 * D) // 4
    row = lambda a: a.reshape(1, -1)
    bf = lambda a: a.astype(BF16)

    xs = x.reshape(N, D)
    for l in range(depth):
        wl = bf(w_in[l])
        c4 = s5w + 4 * mw + 2 * M_HEADS
        ut, qk, v, o, gif, gates = _in_proj(
            xs, row(mix_norm_g[l]), wl, wl[:, c4:], row(b_gate[l]), (s5w, 2 * mw, mw, mw))

        bd, w_st, v_st, a16 = _s5_prep(
            s5_log_dt[l], s5_lam_re[l], s5_lam_im[l], s5_b_re[l], s5_b_im[l], s5_c_re[l],
            s5_c_im[l], s5_d[l])
        yt = _s5_scan(ut, bd, w_st, v_st, a16, B)

        bif = jnp.pad(jnp.concatenate([m_b_i[l], m_b_f[l]]), (0, LANES - 2 * M_HEADS)).reshape(1, LANES)
        y_m = _mlstm(qk, v, o, gif, m_conv_w[l], row(m_conv_b[l]), bif, row(m_norm_g[l]), B)

        qkf, vof = _mem_kv(mem, row(mem_norm_g[l]), bf(x_wkv[l]), bf(x_wq[l]), bf(x_wo[l]))
        x2 = _merge_attn(xs, yt, y_m, gates, bf(s5_w_glu[l]), row(s5_b_glu[l]), bf(w_br_s5[l]),
                         bf(w_br_m[l]), bf(w_out[l]), row(x_norm_g[l]), qkf, vof, B)
        dff = f_w_down.shape[1]
        half = jnp.concatenate([jnp.ones((dff,), F32), jnp.full((dff,), 0.5, F32)])
        xs = _conv_ffn(x2, row(f_norm_g[l]), bf(f_w_up[l]), f_conv_w[l] * half,
                       row(f_conv_b[l] * half), bf(f_w_down[l]), row(final_norm_g), B,
                       l == depth - 1)
    return xs.reshape(B, L, D)
```

```python
import functools
import math

import jax
import jax.numpy as jnp
from jax import lax
from jax.experimental import pallas as pl
from jax.experimental.pallas import tpu as pltpu

F32 = jnp.float32
BF16 = jnp.bfloat16

EPS = 1e-6
S5_GROUP = 16
S5_STATE = 64
S5_CHUNK = 16
M_HEADS = 4
M_CONV = 4
M_CHUNK = 128
M_STEP_CHUNKS = 8
M_PAD = 8
X_HEADS = 4
FFN_CONV = 3
NEG_BIG = -1e30

LANES = 128
S5_BLOCK = LANES // S5_GROUP
TOKEN_TILE = 512
PROJ_TILE = 1024
FFN_CHUNK = 256
FFN_PAD = 8
VMEM_LIMIT = 56 * 1024 * 1024


def _dot(a, b):
    return jnp.dot(a, b, preferred_element_type=F32)


def _dot_nt(a, b):
    return lax.dot_general(a, b, (((1,), (1,)), ((), ())), preferred_element_type=F32)


def _sigmoid(x):
    return 1.0 / (1.0 + jnp.exp(-x))


def _log_sigmoid(x):
    return jnp.minimum(x, 0.0) - jnp.log(1.0 + jnp.exp(-jnp.abs(x)))


GELU_C1 = math.sqrt(2.0 / math.pi)
GELU_C3 = 0.044715 * GELU_C1


def _gelu(x):
    return x * (0.5 * (1.0 + jnp.tanh(x * (GELU_C1 + GELU_C3 * (x * x)))))


def _rmsnorm(x, g):
    return x * lax.rsqrt(jnp.mean(x * x, axis=-1, keepdims=True) + EPS) * g


def _cmul(ar, ai, br, bi):
    return ar * br - ai * bi, ar * bi + ai * br


def _full(shape):
    n = len(shape)
    return pl.BlockSpec(shape, lambda *_: (0,) * n)


def _params(sem):
    return pltpu.CompilerParams(dimension_semantics=sem, vmem_limit_bytes=VMEM_LIMIT)


def _in_proj_kernel(x_ref, g_ref, wu_ref, wqk_ref, wv_ref, wo_ref, wif_ref, wg_ref, bg_ref,
                    ut_ref, qk_ref, v_ref, o_ref, if_ref, gt_ref, us_ref):
    hb = _rmsnorm(x_ref[...], g_ref[...]).astype(BF16)
    u = _dot(hb, wu_ref[...])
    n_slab = u.shape[1] // LANES
    for j in range(n_slab):
        us_ref[j] = u[:, j * LANES:(j + 1) * LANES]
    rows = u.shape[0] // S5_CHUNK
    for t in range(S5_CHUNK):
        for j in range(n_slab):
            ut_ref[t, :, j * LANES:(j + 1) * LANES] = (
                us_ref[j, pl.ds(t, rows, stride=S5_CHUNK), :].astype(BF16))
    qk_ref[...] = _dot(hb, wqk_ref[...]).astype(BF16)
    v_ref[...] = _dot(hb, wv_ref[...]).astype(BF16)
    o_ref[...] = _sigmoid(_dot(hb, wo_ref[...])).astype(BF16)
    if_ref[...] = _dot(hb, wif_ref[...])
    gt_ref[...] = _sigmoid(_dot(hb, wg_ref[...]) + bg_ref[...]).astype(BF16)


def _in_proj(x2d, g, wu, wqk, wv, wo, wif, wg, bg):
    n, d = x2d.shape
    tm = PROJ_TILE
    row = lambda c: pl.BlockSpec((tm, c), lambda i: (i, 0))
    outs = [(wqk.shape[1], BF16), (wv.shape[1], BF16), (wo.shape[1], BF16),
            (wif.shape[1], F32), (wg.shape[1], BF16)]
    s5w = wu.shape[1]
    T = S5_CHUNK
    return pl.pallas_call(
        _in_proj_kernel,
        grid=(n // tm,),
        in_specs=[row(d), _full(g.shape), _full(wu.shape), _full(wqk.shape), _full(wv.shape),
                  _full(wo.shape), _full(wif.shape), _full(wg.shape), _full(bg.shape)],
        out_specs=[pl.BlockSpec((T, tm // T, s5w), lambda i: (0, i, 0))] + [row(c) for c, _ in outs],
        out_shape=([jax.ShapeDtypeStruct((T, n // T, s5w), BF16)]
                   + [jax.ShapeDtypeStruct((n, c), dt) for c, dt in outs]),
        scratch_shapes=[pltpu.VMEM((s5w // LANES, tm, LANES), F32)],
        compiler_params=_params(("parallel",)),
        name="in_proj",
    )(x2d, g, wu, wqk, wv, wo, wif, wg, bg)


def _split3(x):
    hi = x.astype(BF16)
    r1 = x - hi.astype(F32)
    mid = r1.astype(BF16)
    lo = (r1 - mid.astype(F32)).astype(BF16)
    return hi, mid, lo


def _dot_x3(a, b):
    a_hi, a_lo, _ = _split3(a)
    b_hi, b_lo, _ = _split3(b)
    return _dot(a_hi, b_hi) + _dot(a_hi, b_lo) + _dot(a_lo, b_hi)


def _place(x, onehot):
    hi, mid, lo = _split3(x)
    return _dot(hi, onehot) + _dot(mid, onehot) + _dot(lo, onehot)


def _s5_prep_kernel(ldt_r_ref, lre_r_ref, lim_r_ref, ldt_c_ref, lre_c_ref, lim_c_ref,
                    btre_ref, btim_ref, ctre_ref, ctim_ref, d_ref,
                    bd_ref, w_ref, v_ref, a16_ref, pre_ref, pim_ref):
    P, H, T, M = S5_STATE, S5_GROUP, S5_CHUNK, S5_BLOCK
    MP, MH = M * P, M * H

    lr, li = lre_r_ref[0], lim_r_ref[0]
    dt_r = jnp.exp(ldt_r_ref[0])
    e_r = jnp.exp(lr * dt_r)
    ar_r, ai_r = e_r * jnp.cos(li * dt_r), e_r * jnp.sin(li * dt_r)
    dt_c = jnp.exp(ldt_c_ref[0])
    e_c = jnp.exp(lre_c_ref[0] * dt_c)
    ar_c, ai_c = e_c * jnp.cos(lim_c_ref[0] * dt_c), e_c * jnp.sin(lim_c_ref[0] * dt_c)

    t_r = lax.broadcasted_iota(jnp.int32, (P, MP), 0)
    t_l = lax.broadcasted_iota(jnp.int32, (P, MP), 1)
    tile_p = (jnp.bitwise_and(t_l, P - 1) == t_r).astype(BF16)
    b_r = lax.broadcasted_iota(jnp.int32, (MH, MP), 0)
    b_l = lax.broadcasted_iota(jnp.int32, (MH, MP), 1)
    sh_h, sh_p = H.bit_length() - 1, P.bit_length() - 1
    own_b = jnp.right_shift(b_r, sh_h) == jnp.right_shift(b_l, sh_p)
    bre = jnp.where(own_b, _place(btre_ref[0], tile_p), 0.0)
    bim = jnp.where(own_b, _place(btim_ref[0], tile_p), 0.0)
    inv = 1.0 / (lr * lr + li * li)
    cr = ((ar_r - 1.0) * lr + ai_r * li) * inv
    ci = (ai_r * lr - (ar_r - 1.0) * li) * inv
    bre, bim = _cmul(cr, ci, bre, bim)

    u_r = lax.broadcasted_iota(jnp.int32, (H, MH), 0)
    u_l = lax.broadcasted_iota(jnp.int32, (H, MH), 1)
    tile_h = (jnp.bitwise_and(u_l, H - 1) == u_r).astype(BF16)
    c_r = lax.broadcasted_iota(jnp.int32, (MP, MH), 0)
    c_l = lax.broadcasted_iota(jnp.int32, (MP, MH), 1)
    own_c = jnp.right_shift(c_r, sh_p) == jnp.right_shift(c_l, sh_h)
    cre = jnp.where(own_c, _place(ctre_ref[0], tile_h), 0.0)
    cim = jnp.where(own_c, _place(ctim_ref[0], tile_h), 0.0)

    qre, qim = cre, cim
    for tau in range(T + 1):
        pre_ref[:, tau * MH:(tau + 1) * MH] = qre
        pim_ref[:, tau * MH:(tau + 1) * MH] = qim
        if tau < T:
            qre, qim = _cmul(ar_c, ai_c, qre, qim)

    bd = _dot_x3(bre, pre_ref[:, 0:T * MH]) - _dot_x3(bim, pim_ref[:, 0:T * MH])
    d_r = lax.broadcasted_iota(jnp.int32, (MH, MH), 0)
    d_l = lax.broadcasted_iota(jnp.int32, (MH, MH), 1)
    bd_ref[0, :, 0:MH] = (bd[:, 0:MH] + jnp.where(d_r == d_l, d_ref[0], 0.0)).astype(BF16)
    bd_ref[0, :, MH:T * MH] = bd[:, MH:T * MH].astype(BF16)

    v_ref[0, 0:MP, :] = pre_ref[:, MH:(T + 1) * MH].astype(BF16)
    v_ref[0, MP:2 * MP, :] = (-pim_ref[:, MH:(T + 1) * MH]).astype(BF16)

    wre, wim = bre, bim
    for t in range(T - 1, -1, -1):
        w_ref[0, t * MH:(t + 1) * MH, 0:MP] = wre.astype(BF16)
        w_ref[0, t * MH:(t + 1) * MH, MP:2 * MP] = wim.astype(BF16)
        if t > 0:
            wre, wim = _cmul(ar_r, ai_r, wre, wim)

    sr, si = ar_r, ai_r
    for _ in range(4):
        sr, si = _cmul(sr, si, sr, si)
    row8 = lax.broadcasted_iota(jnp.int32, (8, MP), 0)
    a16_ref[0] = jnp.where(row8 == 0, sr, si)


def _s5_prep(log_dt, lam_re, lam_im, b_re, b_im, c_re, c_im, d):
    G, P = lam_re.shape
    H, T, M = S5_GROUP, S5_CHUNK, S5_BLOCK
    J = G // M
    MP, MH = M * P, M * H
    ldt = jnp.repeat(log_dt, P)
    args = [
        ldt.reshape(J, 1, MP), lam_re.reshape(J, 1, MP), lam_im.reshape(J, 1, MP),
        ldt.reshape(J, MP, 1), lam_re.reshape(J, MP, 1), lam_im.reshape(J, MP, 1),
        b_re.transpose(0, 2, 1).reshape(J, MH, P), b_im.transpose(0, 2, 1).reshape(J, MH, P),
        c_re.transpose(0, 2, 1).reshape(J, MP, H), c_im.transpose(0, 2, 1).reshape(J, MP, H),
        d.reshape(J, 1, MH),
    ]
    blk = lambda s: pl.BlockSpec((1,) + tuple(s[1:]), lambda j: (j, 0, 0))
    outs = [((J, MH, T * MH), BF16), ((J, T * MH, 2 * MP), BF16), ((J, 2 * MP, T * MH), BF16),
            ((J, 8, MP), F32)]
    return pl.pallas_call(
        _s5_prep_kernel,
        grid=(J,),
        in_specs=[blk(a.shape) for a in args],
        out_specs=[blk(s) for s, _ in outs],
        out_shape=[jax.ShapeDtypeStruct(s, dt) for s, dt in outs],
        scratch_shapes=[pltpu.VMEM((MP, (T + 1) * MH), F32)] * 2,
        compiler_params=_params(("parallel",)),
        name="s5_prep",
    )(*args)


def _s5_scan_kernel(ut_ref, bd_ref, w_ref, v_ref, a16_ref, yt_ref, big_ref, xs_ref, sp_ref,
                    part_ref):
    T = S5_CHUNK
    R = ut_ref.shape[1]
    MH = ut_ref.shape[2]
    MP = a16_ref.shape[2]

    @pl.when(pl.program_id(1) == 0)
    def _():
        for s in range(T):
            for t in range(T):
                blk = (bd_ref[0, :, (t - s) * MH:(t - s + 1) * MH] if t >= s
                       else jnp.zeros((MH, MH), BF16))
                big_ref[s * MH:(s + 1) * MH, t * MH:(t + 1) * MH] = blk

    x = jnp.concatenate([ut_ref[t] for t in range(T)], axis=1)
    xs_ref[...] = _dot(x, w_ref[0])
    a16 = a16_ref[0]
    a1 = (jnp.broadcast_to(a16[0:1], (8, MP)), jnp.broadcast_to(a16[1:2], (8, MP)))
    a2 = _cmul(*a1, *a1)
    a4 = _cmul(*a2, *a2)
    rid = lax.broadcasted_iota(jnp.int32, (8, MP), 0)
    pw = a1
    for r in range(1, 8):
        nxt = _cmul(*pw, *a1)
        pw = (jnp.where(rid >= r, nxt[0], pw[0]), jnp.where(rid >= r, nxt[1], pw[1]))

    def shift_rows(v, n):
        return jnp.where(rid >= n, pltpu.roll(v, n, 0), 0.0)

    def body(i, carry):
        sr, si = carry
        r0 = i * 8
        yr = xs_ref[r0:r0 + 8, 0:MP]
        yi = xs_ref[r0:r0 + 8, MP:2 * MP]
        for n, am in ((1, a1), (2, a2), (4, a4)):
            tr, ti = _cmul(*am, shift_rows(yr, n), shift_rows(yi, n))
            yr, yi = yr + tr, yi + ti
        cr, ci = _cmul(*pw, jnp.broadcast_to(sr, (8, MP)), jnp.broadcast_to(si, (8, MP)))
        er, ei = yr + cr, yi + ci
        sp_ref[r0:r0 + 8, 0:MP] = jnp.where(rid == 0, sr, pltpu.roll(er, 1, 0))
        sp_ref[r0:r0 + 8, MP:2 * MP] = jnp.where(rid == 0, si, pltpu.roll(ei, 1, 0))
        return er[7:8], ei[7:8]

    for p in range(T // 2):
        k = 2 * (p + 1) * MH
        cols = slice(2 * p * MH, 2 * (p + 1) * MH)
        part_ref[:, cols] = _dot(x[:, 0:k], big_ref[0:k, cols])

    carry = (jnp.zeros((1, MP), F32), jnp.zeros((1, MP), F32))
    for i in range(R // 8):
        carry = body(i, carry)

    sp = sp_ref[...].astype(BF16)
    for p in range(T // 2):
        cols = slice(2 * p * MH, 2 * (p + 1) * MH)
        acc = part_ref[:, cols] + _dot(sp, v_ref[0, :, cols])
        yt_ref[2 * p] = acc[:, 0:MH].astype(BF16)
        yt_ref[2 * p + 1] = acc[:, MH:2 * MH].astype(BF16)


def _s5_scan(ut, bd, w, v, a16, n_batch):
    T, nrow, s5w = ut.shape
    R = nrow // n_batch
    MH = LANES
    MP = a16.shape[2]
    per_j = lambda s: pl.BlockSpec((1,) + tuple(s[1:]), lambda j, b: (j, 0, 0))
    io = pl.BlockSpec((T, R, MH), lambda j, b: (0, b, j))
    return pl.pallas_call(
        _s5_scan_kernel,
        grid=(s5w // MH, n_batch),
        in_specs=[io, per_j(bd.shape), per_j(w.shape), per_j(v.shape), per_j(a16.shape)],
        out_specs=io,
        out_shape=jax.ShapeDtypeStruct(ut.shape, BF16),
        scratch_shapes=[pltpu.VMEM((T * MH, T * MH), BF16),
                        pltpu.VMEM((R, 2 * MP), F32),
                        pltpu.VMEM((R, 2 * MP), F32),
                        pltpu.VMEM((R, T * MH), F32)],
        compiler_params=_params(("arbitrary", "arbitrary")),
        name="s5_scan",
    )(ut, bd, w, v, a16)


def _mlstm_kernel(qk_ref, v_ref, o_ref, if_ref, cw_ref, cb_ref, bif_ref, ng_ref, y_ref,
                  cv_ref, cn_ref, m_ref):
    Lc = M_CHUNK
    S = qk_ref.shape[0]
    W = v_ref.shape[1]
    Dh = W // M_HEADS
    base = 2 * M_PAD

    @pl.when(pl.program_id(1) == 0)
    def _():
        cv_ref[:, 0:base, :] = jnp.zeros((cv_ref.shape[0], base, LANES), F32)
        cn_ref[...] = jnp.zeros(cn_ref.shape, F32)
        m_ref[...] = jnp.zeros(m_ref.shape, F32)

    @pl.when(pl.program_id(1) > 0)
    def _():
        cv_ref[:, 0:base, :] = cv_ref[:, 2 * S:2 * S + base, :]

    x = qk_ref[...].astype(F32)
    n_slab = x.shape[1] // LANES
    for s in range(n_slab):
        cv_ref[s // 2, pl.ds(base + s % 2, S, stride=2), :] = x[:, s * LANES:(s + 1) * LANES]
    cols = []
    for s in range(n_slab):
        l0 = s * LANES
        acc = cb_ref[:, l0:l0 + LANES]
        for k in range(M_CONV):
            start = base + s % 2 - 2 * (M_CONV - 1 - k)
            acc = acc + cw_ref[k:k + 1, l0:l0 + LANES] * cv_ref[s // 2, pl.ds(start, S, stride=2), :]
        cols.append(acc * _sigmoid(acc))
    qk = jnp.concatenate(cols, axis=1)

    n_ci = S // Lc
    inst = [(ci, h) for ci in range(n_ci) for h in range(M_HEADS)]
    rows_of = lambda b: slice(b * Lc, (b + 1) * Lc)
    stack = lambda parts: jnp.concatenate(parts, axis=0)
    rep = lambda parts: stack([jnp.broadcast_to(p, (Lc, Dh)) for p in parts])

    lane = lax.broadcasted_iota(jnp.int32, (Lc, Lc), 1)
    row = lax.broadcasted_iota(jnp.int32, (Lc, Lc), 0)
    causal1 = row >= lane
    tril = causal1.astype(BF16)
    ones_blk = jnp.ones((Lc, Dh), BF16)

    gm, bcu, gm_t, bcu_t = [], [], [], []
    for ci in range(n_ci):
        pre = if_ref[ci * Lc:(ci + 1) * Lc, :] + bif_ref[...]
        g = jnp.where(lane < M_HEADS, pre, jnp.where(lane < 2 * M_HEADS, _log_sigmoid(pre), 0.0))
        c = _place_left(tril, g)
        gm.append(g)
        bcu.append(c)
        gm_t.append(g.T)
        bcu_t.append(c.T)

    hs = lambda h: slice(h * Dh, (h + 1) * Dh)
    cs = lambda ci: slice(ci * Lc, (ci + 1) * Lc)
    q = stack([qk[cs(ci), hs(h)] for ci, h in inst])
    k = stack([qk[cs(ci), W + h * Dh:W + (h + 1) * Dh] for ci, h in inst]) * (Dh ** -0.5)
    v1 = [jnp.concatenate([v_ref[cs(ci), hs(h)], ones_blk], axis=1) for ci, h in inst]
    ig = rep([gm[ci][:, h:h + 1] for ci, h in inst])
    bc = rep([bcu[ci][:, M_HEADS + h:M_HEADS + h + 1] for ci, h in inst])
    g_c = [jnp.broadcast_to(bcu[ci][Lc - 1:Lc, M_HEADS + h:M_HEADS + h + 1], (1, Dh))
           for ci, h in inst]
    r_mat = stack([jnp.where(causal1, gm_t[ci][h:h + 1, :] - bcu_t[ci][M_HEADS + h:M_HEADS + h + 1, :],
                             NEG_BIG) for ci, h in inst])

    a_end = rep(g_c) - bc + ig
    m_loc = [jnp.max(a_end[rows_of(b)], axis=0, keepdims=True) for b in range(len(inst))]
    m_run = [m_ref[h][0:1, :] for h in range(M_HEADS)]
    m_prev, m_new = [], []
    for b, (ci, h) in enumerate(inst):
        m_prev.append(m_run[h])
        m_run[h] = jnp.maximum(g_c[b] + m_run[h], m_loc[b])
        m_new.append(m_run[h])
    w_rep = jnp.exp(a_end - rep(m_new))

    mp = rep(m_prev)
    mx = jnp.maximum(mp, jnp.broadcast_to(jnp.max(r_mat, axis=1, keepdims=True), mp.shape))
    m_t = bc + mx
    qb, kb = q.astype(BF16), k.astype(BF16)
    s_mat = stack([_dot_nt(qb[rows_of(b)], kb[rows_of(b)]) for b in range(len(inst))])
    s_mat = s_mat * jnp.exp(r_mat - mx)
    inter_w = jnp.exp(mp - mx)
    lhs = jnp.concatenate([s_mat.astype(BF16), (inter_w * q).astype(BF16)], axis=1)

    kw = k * w_rep
    upd = [_dot(kw[rows_of(b)].T.astype(BF16), v1[b]) for b in range(len(inst))]
    cn_run = [cn_ref[h] for h in range(M_HEADS)]
    nd = []
    for b, (ci, h) in enumerate(inst):
        rhs = jnp.concatenate([v1[b], cn_run[h].astype(BF16)], axis=0)
        nd.append(_dot(lhs[rows_of(b)], rhs))
        decay = jnp.exp(g_c[b] + m_prev[b] - m_new[b])
        cn_run[h] = jnp.concatenate([decay, decay], axis=1) * cn_run[h] + upd[b]
    nd = stack(nd)
    num = nd[:, :Dh]
    den = nd[:, Dh:]
    r = 1.0 / jnp.maximum(jnp.abs(den), jnp.exp(-m_t))
    z = stack([o_ref[cs(ci), hs(h)] for ci, h in inst]).astype(F32) * num
    zz_hi, zz_lo, _ = _split3(z * z)
    mz = (_dot(zz_hi, ones_blk) + _dot(zz_lo, ones_blk)) * (1.0 / Dh)
    hh = z * (r * lax.rsqrt(r * r * mz + EPS))
    for b, (ci, h) in enumerate(inst):
        y_ref[cs(ci), hs(h)] = (hh[rows_of(b)] * ng_ref[:, hs(h)]).astype(BF16)

    for h in range(M_HEADS):
        cn_ref[h] = cn_run[h]
        m_ref[h] = jnp.broadcast_to(m_run[h], m_ref.shape[1:])


def _place_left(onehot, x):
    hi, mid, lo = _split3(x)
    return _dot(onehot, hi) + _dot(onehot, mid) + _dot(onehot, lo)


def _mlstm(qk, v, o, gif, conv_w, conv_b, bif, norm_g, n_batch):
    n, w = v.shape
    rows = M_STEP_CHUNKS * M_CHUNK
    nc = n // n_batch // rows
    dh = w // M_HEADS
    row = lambda c: pl.BlockSpec((rows, c), lambda b, i: (b * nc + i, 0))
    return pl.pallas_call(
        _mlstm_kernel,
        grid=(n_batch, nc),
        in_specs=[row(2 * w), row(w), row(w), row(gif.shape[1]), _full(conv_w.shape),
                  _full(conv_b.shape), _full(bif.shape), _full(norm_g.shape)],
        out_specs=row(w),
        out_shape=jax.ShapeDtypeStruct((n, w), BF16),
        scratch_shapes=[pltpu.VMEM((2 * w // LANES // 2, 2 * (M_PAD + rows), LANES), F32),
                        pltpu.VMEM((M_HEADS, dh, 2 * dh), F32),
                        pltpu.VMEM((M_HEADS, 8, LANES), F32)],
        compiler_params=_params(("arbitrary", "arbitrary")),
        name="mlstm",
    )(qk, v, o, gif, conv_w, conv_b, bif, norm_g)


def _mem_kv_kernel(mem_ref, g_ref, wkv_ref, wq_ref, wo_ref, qk_ref, vo_ref):
    d = mem_ref.shape[2]
    n_mem = mem_ref.shape[1]
    dh = d // X_HEADS
    kv = _dot(_rmsnorm(mem_ref[0], g_ref[...]).astype(BF16), wkv_ref[...]).astype(BF16)
    for h in range(X_HEADS):
        k_h = kv[:, h * dh:(h + 1) * dh]
        v_h = kv[:, d + h * dh:d + (h + 1) * dh]
        qk_ref[0, :, h * n_mem:(h + 1) * n_mem] = (
            _dot_nt(wq_ref[:, h * dh:(h + 1) * dh], k_h) * (dh ** -0.5)).astype(BF16)
        vo_ref[0, h * n_mem:(h + 1) * n_mem, :] = _dot(v_h, wo_ref[h * dh:(h + 1) * dh, :]).astype(BF16)


def _mem_kv(mem, g, wkv, wq, wo):
    nb, n_mem, d = mem.shape
    hm = X_HEADS * n_mem
    return pl.pallas_call(
        _mem_kv_kernel,
        grid=(nb,),
        in_specs=[pl.BlockSpec((1, n_mem, d), lambda b: (b, 0, 0)), _full(g.shape), _full(wkv.shape),
                  _full(wq.shape), _full(wo.shape)],
        out_specs=[pl.BlockSpec((1, d, hm), lambda b: (b, 0, 0)),
                   pl.BlockSpec((1, hm, d), lambda b: (b, 0, 0))],
        out_shape=[jax.ShapeDtypeStruct((nb, d, hm), BF16), jax.ShapeDtypeStruct((nb, hm, d), BF16)],
        compiler_params=_params(("arbitrary",)),
        name="mem_kv",
    )(mem, g, wkv, wq, wo)


def _merge_attn_kernel(x_ref, yt_ref, ym_ref, gt_ref, wglu_ref, bglu_ref, wbs_ref, wbm_ref,
                       wout_ref, xg_ref, qkf_ref, vof_ref, x2_ref, ys_ref):
    d = x_ref.shape[1]
    rows = yt_ref.shape[1]
    n_slab = yt_ref.shape[2] // LANES
    p_m = _dot(ym_ref[...], wbm_ref[...])
    for t in range(S5_CHUNK):
        for j in range(n_slab):
            ys_ref[j, pl.ds(t, rows, stride=S5_CHUNK), :] = (
                yt_ref[t, :, j * LANES:(j + 1) * LANES].astype(F32))
    ys = _gelu(jnp.concatenate([ys_ref[j] for j in range(n_slab)], axis=1))
    ys = ys * _sigmoid(_dot(ys.astype(BF16), wglu_ref[...]) + bglu_ref[...])
    p_s = _dot(ys.astype(BF16), wbs_ref[...])
    merged = gt_ref[:, :d].astype(F32) * p_s + gt_ref[:, d:].astype(F32) * p_m
    x1 = x_ref[...] + _dot(merged.astype(BF16), wout_ref[...])

    s_all = _dot(_rmsnorm(x1, xg_ref[...]).astype(BF16), qkf_ref[0])
    n_mem = s_all.shape[1] // X_HEADS
    probs = []
    for h in range(X_HEADS):
        s = s_all[:, h * n_mem:(h + 1) * n_mem]
        p = jnp.exp(s - jnp.max(s, axis=-1, keepdims=True))
        probs.append((p / jnp.sum(p, axis=-1, keepdims=True)).astype(BF16))
    x2_ref[...] = x1 + _dot(jnp.concatenate(probs, axis=1), vof_ref[0])


def _merge_attn(x2d, yt, ym, gt, wglu, bglu, wbs, wbm, wout, xg, qkf, vof, n_batch):
    n, d = x2d.shape
    tm = PROJ_TILE
    per_b = n // n_batch // tm
    row = lambda c: pl.BlockSpec((tm, c), lambda b, i: (b * per_b + i, 0))
    per_batch = lambda a: pl.BlockSpec((1,) + a.shape[1:], lambda b, i: (b, 0, 0))
    T, _, s5w = yt.shape
    return pl.pallas_call(
        _merge_attn_kernel,
        grid=(n_batch, per_b),
        in_specs=[row(d), pl.BlockSpec((T, tm // T, s5w), lambda b, i: (0, b * per_b + i, 0)),
                  row(ym.shape[1]), row(gt.shape[1]),
                  _full(wglu.shape), _full(bglu.shape), _full(wbs.shape), _full(wbm.shape),
                  _full(wout.shape), _full(xg.shape), per_batch(qkf), per_batch(vof)],
        out_specs=row(d),
        out_shape=jax.ShapeDtypeStruct((n, d), F32),
        scratch_shapes=[pltpu.VMEM((s5w // LANES, tm, LANES), F32)],
        compiler_params=_params(("parallel", "parallel")),
        name="merge_attn",
    )(x2d, yt, ym, gt, wglu, bglu, wbs, wbm, wout, xg, qkf, vof)


def _conv_ffn_kernel(x_ref, fg_ref, wup_ref, cw_ref, cb_ref, wdn_ref, ng_ref, o_ref,
                     up_ref, act_ref, *, final_norm):
    tm = x_ref.shape[0]
    dff = wdn_ref.shape[0]
    ck = FFN_CHUNK
    n_chunks = dff // ck
    spc = ck // LANES
    base = 2 * FFN_PAD

    @pl.when(pl.program_id(1) == 0)
    def _():
        up_ref[:, 0:base, :] = jnp.zeros((up_ref.shape[0], base, LANES), F32)

    @pl.when(pl.program_id(1) > 0)
    def _():
        up_ref[:, 0:base, :] = up_ref[:, 2 * tm:2 * tm + base, :]

    x2 = x_ref[...]
    hb = _rmsnorm(x2, fg_ref[...]).astype(BF16)

    def up(c, half):
        c0 = half * dff + c * ck
        u = _dot(hb, wup_ref[:, c0:c0 + ck])
        for s in range(spc):
            up_ref[c * spc + s, pl.ds(base + half, tm, stride=2), :] = u[:, s * LANES:(s + 1) * LANES]

    def conv(c, half):
        c0 = half * dff + c * ck
        cols = []
        for s in range(spc):
            l0 = c0 + s * LANES
            acc = cb_ref[:, l0:l0 + LANES]
            for k in range(FFN_CONV):
                start = base + half - 2 * (FFN_CONV - 1 - k)
                acc = acc + (cw_ref[k:k + 1, l0:l0 + LANES]
                             * up_ref[c * spc + s, pl.ds(start, tm, stride=2), :])
            cols.append(acc)
        return jnp.concatenate(cols, axis=1)

    up(0, 0)
    up(0, 1)
    for c in range(n_chunks):
        if c + 1 < n_chunks:
            up(c + 1, 0)
            up(c + 1, 1)
        a = conv(c, 0)
        ab = a * conv(c, 1)
        th = jnp.tanh(a * (GELU_C1 + GELU_C3 * (a * a)))
        act_ref[:, c * ck:(c + 1) * ck] = (ab + ab * th).astype(BF16)
    hr = tm // 2
    for r in range(2):
        y = x_ref[r * hr:(r + 1) * hr, :] + _dot(act_ref[r * hr:(r + 1) * hr, :], wdn_ref[...])
        o_ref[r * hr:(r + 1) * hr, :] = _rmsnorm(y, ng_ref[...]) if final_norm else y


def _conv_ffn(x2, fg, wup, cw, cb, wdn, ng, n_batch, final_norm):
    n, d = x2.shape
    tm = TOKEN_TILE
    per_b = n // n_batch // tm
    row = pl.BlockSpec((tm, d), lambda b, i: (b * per_b + i, 0))
    return pl.pallas_call(
        functools.partial(_conv_ffn_kernel, final_norm=final_norm),
        grid=(n_batch, per_b),
        in_specs=[row, _full(fg.shape), _full(wup.shape), _full(cw.shape), _full(cb.shape),
                  _full(wdn.shape), _full(ng.shape)],
        out_specs=row,
        out_shape=jax.ShapeDtypeStruct((n, d), F32),
        scratch_shapes=[pltpu.VMEM((wdn.shape[0] // LANES, 2 * (FFN_PAD + tm), LANES), F32),
                        pltpu.VMEM((tm, wdn.shape[0]), BF16)],
        compiler_params=_params(("arbitrary", "arbitrary")),
        name="conv_ffn",
    )(x2, fg, wup, cw, cb, wdn, ng)


def kernel(x, mem, mix_norm_g, w_in, s5_lam_re, s5_lam_im, s5_b_re, s5_b_im, s5_c_re, s5_c_im,
           s5_d, s5_log_dt, s5_w_glu, s5_b_glu, m_conv_w, m_conv_b, m_b_i, m_b_f, m_norm_g,
           w_br_s5, w_br_m, b_gate, w_out, x_norm_g, mem_norm_g, x_wq, x_wkv, x_wo,
           f_norm_g, f_w_up, f_conv_w, f_conv_b, f_w_down, final_norm_g):
    B, L, D = x.shape
    N = B * L
    depth = w_in.shape[0]
    G = s5_lam_re.shape[1]
    s5w = G * S5_GROUP
    mw = (w_in.shape[2] - s5w - 2 * M_HEADS - 2 * D) // 4
    row = lambda a: a.reshape(1, -1)
    bf = lambda a: a.astype(BF16)

    xs = x.reshape(N, D)
    for l in range(depth):
        wl = w_in[l]
        c0 = s5w
        c1 = c0 + 2 * mw
        c2 = c1 + mw
        c3 = c2 + mw
        c4 = c3 + 2 * M_HEADS
        wif = jnp.pad(wl[:, c3:c4], ((0, 0), (0, LANES - 2 * M_HEADS)))
        ut, qk, v, o, gif, gates = _in_proj(
            xs, row(mix_norm_g[l]), bf(wl[:, :c0]), bf(wl[:, c0:c1]), bf(wl[:, c1:c2]),
            bf(wl[:, c2:c3]), bf(wif), bf(wl[:, c4:]), row(b_gate[l]))

        bd, w_st, v_st, a16 = _s5_prep(
            s5_log_dt[l], s5_lam_re[l], s5_lam_im[l], s5_b_re[l], s5_b_im[l], s5_c_re[l],
            s5_c_im[l], s5_d[l])
        yt = _s5_scan(ut, bd, w_st, v_st, a16, B)

        bif = jnp.pad(jnp.concatenate([m_b_i[l], m_b_f[l]]), (0, LANES - 2 * M_HEADS)).reshape(1, LANES)
        y_m = _mlstm(qk, v, o, gif, m_conv_w[l], row(m_conv_b[l]), bif, row(m_norm_g[l]), B)

        qkf, vof = _mem_kv(mem, row(mem_norm_g[l]), bf(x_wkv[l]), bf(x_wq[l]), bf(x_wo[l]))
        x2 = _merge_attn(xs, yt, y_m, gates, bf(s5_w_glu[l]), row(s5_b_glu[l]), bf(w_br_s5[l]),
                         bf(w_br_m[l]), bf(w_out[l]), row(x_norm_g[l]), qkf, vof, B)
        dff = f_w_down.shape[1]
        half = jnp.concatenate([jnp.ones((dff,), F32), jnp.full((dff,), 0.5, F32)])
        xs = _conv_ffn(x2, row(f_norm_g[l]), bf(f_w_up[l]), f_conv_w[l] * half,
                       row(f_conv_b[l] * half), bf(f_w_down[l]), row(final_norm_g), B,
                       l == depth - 1)
    return xs.reshape(B, L, D)
```
